```python
import jax, jax.numpy as jnp
from jax import lax
import numpy as np

D_MODEL = 1024
BATCH = 8
SEQ = 8192
DEPTH = 1

CHUNK = 64
HEAD_DIM = 64
N_HEADS_GDN = 8
N_HEADS_ATT = 8
D_GDN = N_HEADS_GDN * HEAD_DIM
D_ATT = N_HEADS_ATT * HEAD_DIM
D_MIX = D_GDN + D_ATT
CONV_WIDTH = 4
BAND_CHUNKS = 8
REL_CLIP = 128
N_REL = 2 * REL_CLIP + 1
D_FF = 2816
RMS_EPS = 1e-6
L2_EPS = 1e-6
NEG_INF = -1e30
D_IN_MIX = 4 * D_GDN + 2 * N_HEADS_GDN + 3 * D_ATT

kernel_name = "hymba_gdn_chunkattn_macaron"


def rmsnorm(x, g):
    xf = x.astype(jnp.float32)
    y = xf * lax.rsqrt(jnp.mean(xf * xf, axis=-1, keepdims=True) + RMS_EPS)
    return (y * g.astype(jnp.float32)).astype(x.dtype)


def swiglu_ffn(x, w_in, w_out):
    gate, up = jnp.split(x @ w_in, 2, axis=-1)
    return (jax.nn.silu(gate) * up) @ w_out


def causal_depthwise_conv(x, w):
    y = lax.conv_general_dilated(
        x, w[:, None, :].astype(x.dtype), window_strides=(1,),
        padding=[(CONV_WIDTH - 1, 0)], dimension_numbers=("NWC", "WIO", "NWC"),
        feature_group_count=x.shape[-1])
    return y


def l2norm(x):
    return x * lax.rsqrt(jnp.sum(x * x, axis=-1, keepdims=True) + L2_EPS)


def to_chunks(t, n_chunks):
    b, s, h = t.shape[:3]
    t = t.reshape((b, n_chunks, CHUNK, h) + t.shape[3:])
    perm = (0, 3, 1, 2) + tuple(range(4, t.ndim))
    return t.transpose(perm)


def gated_deltanet(qkv, z, a, b, A_log, dt_bias, gdn_norm):
    bsz, seq, _ = qkv.shape
    n = seq // CHUNK
    f32 = jnp.float32
    q, k, v = jnp.split(qkv.astype(f32), 3, axis=-1)
    q = l2norm(q.reshape(bsz, seq, N_HEADS_GDN, HEAD_DIM)) * (HEAD_DIM ** -0.5)
    k = l2norm(k.reshape(bsz, seq, N_HEADS_GDN, HEAD_DIM))
    v = v.reshape(bsz, seq, N_HEADS_GDN, HEAD_DIM)
    beta = jax.nn.sigmoid(b.astype(f32))
    g = -jnp.exp(A_log.astype(f32)) * jax.nn.softplus(a.astype(f32) + dt_bias.astype(f32))

    q, k, v = to_chunks(q, n), to_chunks(k, n), to_chunks(v, n)
    beta, g = to_chunks(beta, n), to_chunks(g, n)
    gc = jnp.cumsum(g, axis=-1)

    lower = jnp.tril(jnp.ones((CHUNK, CHUNK), dtype=bool))
    strict = jnp.tril(jnp.ones((CHUNK, CHUNK), dtype=bool), k=-1)
    eye = jnp.eye(CHUNK, dtype=f32)
    diff = gc[..., :, None] - gc[..., None, :]
    decay = jnp.where(lower, jnp.exp(jnp.where(lower, diff, 0.0)), 0.0)

    kb = k * beta[..., None]
    vb = v * beta[..., None]
    L = jnp.where(strict, jnp.einsum("bhnid,bhnjd->bhnij", kb, k) * decay, 0.0)
    T = lax.linalg.triangular_solve(eye + L, jnp.broadcast_to(eye, L.shape),
                                    left_side=True, lower=True)
    u = jnp.einsum("bhnij,bhnjd->bhnid", T, vb)
    w = jnp.einsum("bhnij,bhnjd->bhnid", T, kb * jnp.exp(gc)[..., None])
    a_qk = jnp.einsum("bhnid,bhnjd->bhnij", q, k) * decay

    def step(S, inp):
        q_i, k_i, u_i, w_i, gc_i, aqk_i = inp
        v_new = u_i - jnp.einsum("bhcd,bhde->bhce", w_i, S)
        o = (jnp.einsum("bhcd,bhde->bhce", q_i * jnp.exp(gc_i)[..., None], S)
             + jnp.einsum("bhij,bhje->bhie", aqk_i, v_new))
        g_last = gc_i[..., -1]
        S = (S * jnp.exp(g_last)[..., None, None]
             + jnp.einsum("bhcd,bhce->bhde", k_i * jnp.exp(g_last[..., None] - gc_i)[..., None], v_new))
        return S, o

    xs = tuple(jnp.moveaxis(t, 2, 0) for t in (q, k, u, w, gc, a_qk))
    S0 = jnp.zeros((bsz, N_HEADS_GDN, HEAD_DIM, HEAD_DIM), f32)
    _, o = lax.scan(step, S0, xs)
    o = o.transpose(1, 0, 3, 2, 4).reshape(bsz, seq, N_HEADS_GDN, HEAD_DIM)
    o = rmsnorm(o, gdn_norm) * jax.nn.silu(z.astype(f32).reshape(bsz, seq, N_HEADS_GDN, HEAD_DIM))
    return o.reshape(bsz, seq, D_GDN).astype(qkv.dtype)


def chunk_attention(q, k, v, rel_bias):
    bsz, seq, _ = q.shape
    n = seq // CHUNK
    band = (BAND_CHUNKS + 1) * CHUNK
    pad = BAND_CHUNKS * CHUNK
    qh = q.reshape(bsz, seq, N_HEADS_ATT, HEAD_DIM)
    kp = jnp.pad(k.reshape(bsz, seq, N_HEADS_ATT, HEAD_DIM), ((0, 0), (pad, 0), (0, 0), (0, 0)))
    vp = jnp.pad(v.reshape(bsz, seq, N_HEADS_ATT, HEAD_DIM), ((0, 0), (pad, 0), (0, 0), (0, 0)))
    rel = jnp.arange(CHUNK)[:, None] - (jnp.arange(band)[None, :] - pad)
    rel_idx = jnp.clip(rel, -REL_CLIP, REL_CLIP) + REL_CLIP
    bias = rel_bias.astype(jnp.float32)[:, rel_idx]
    key_chunk_offset = jnp.arange(band) // CHUNK - BAND_CHUNKS
    scale = HEAD_DIM ** -0.5

    def one_chunk(c):
        start = c * CHUNK
        q_c = lax.dynamic_slice_in_dim(qh, start, CHUNK, axis=1)
        k_c = lax.dynamic_slice_in_dim(kp, start, band, axis=1)
        v_c = lax.dynamic_slice_in_dim(vp, start, band, axis=1)
        s = jnp.einsum("bqhd,bkhd->bhqk", q_c, k_c).astype(jnp.float32) * scale + bias
        valid = (c + key_chunk_offset) >= 0
        s = jnp.where(valid[None, None, None, :], s, NEG_INF)
        p = jax.nn.softmax(s, axis=-1).astype(v_c.dtype)
        return jnp.einsum("bhqk,bkhd->bqhd", p, v_c)

    out = lax.map(one_chunk, jnp.arange(n))
    return out.transpose(1, 0, 2, 3, 4).reshape(bsz, seq, D_ATT)


def setup_inputs(seed: int = 0) -> dict:
    key = jax.random.key(seed)
    ks = jax.random.split(key, 17)
    f32 = jnp.float32

    def dense(k, shape, fan_in):
        return jax.random.normal(k, shape, f32) * (fan_in ** -0.5)

    def gain(k, shape):
        return 1.0 + 0.01 * jax.random.normal(k, shape, f32)

    dt = jnp.exp(jax.random.uniform(ks[8], (DEPTH, N_HEADS_GDN), f32,
                                    minval=np.log(1e-3), maxval=np.log(1e-1)))
    return {
        "x": jax.random.normal(ks[0], (BATCH, SEQ, D_MODEL), f32),
        "ffn1_norm": gain(ks[1], (DEPTH, D_MODEL)),
        "ffn1_w_in": dense(ks[2], (DEPTH, D_MODEL, 2 * D_FF), D_MODEL),
        "ffn1_w_out": dense(ks[3], (DEPTH, D_FF, D_MODEL), D_FF),
        "mix_norm": gain(ks[4], (DEPTH, D_MODEL)),
        "w_in_mix": dense(ks[5], (DEPTH, D_MODEL, D_IN_MIX), D_MODEL),
        "conv_w": dense(ks[6], (DEPTH, CONV_WIDTH, 3 * D_GDN), CONV_WIDTH),
        "A_log": jnp.log(jax.random.uniform(ks[7], (DEPTH, N_HEADS_GDN), f32, minval=1.0, maxval=16.0)),
        "dt_bias": dt + jnp.log(-jnp.expm1(-dt)),
        "gdn_norm": gain(ks[9], (DEPTH, HEAD_DIM)),
        "rel_bias": 0.1 * jax.random.normal(ks[10], (DEPTH, N_HEADS_ATT, N_REL), f32),
        "w_out_mix": dense(ks[11], (DEPTH, D_MIX, D_MODEL), D_MIX),
        "ffn2_norm": gain(ks[12], (DEPTH, D_MODEL)),
        "ffn2_w_in": dense(ks[13], (DEPTH, D_MODEL, 2 * D_FF), D_MODEL),
        "ffn2_w_out": dense(ks[14], (DEPTH, D_FF, D_MODEL), D_FF),
        "final_norm": gain(ks[15], (D_MODEL,)),
    }


def reference(x, ffn1_norm, ffn1_w_in, ffn1_w_out, mix_norm, w_in_mix, conv_w, A_log, dt_bias,
              gdn_norm, rel_bias, w_out_mix, ffn2_norm, ffn2_w_in, ffn2_w_out, final_norm):
    split_pts = [3 * D_GDN, 4 * D_GDN, 4 * D_GDN + N_HEADS_GDN, 4 * D_GDN + 2 * N_HEADS_GDN,
                 4 * D_GDN + 2 * N_HEADS_GDN + D_ATT, 4 * D_GDN + 2 * N_HEADS_GDN + 2 * D_ATT]
    for l in range(DEPTH):
        x = x + 0.5 * swiglu_ffn(rmsnorm(x, ffn1_norm[l]), ffn1_w_in[l], ffn1_w_out[l])
        h = rmsnorm(x, mix_norm[l])
        proj = h @ w_in_mix[l]
        gdn_qkv, gdn_z, gdn_a, gdn_b, att_q, att_k, att_v = jnp.split(proj, split_pts, axis=-1)
        gdn_qkv = jax.nn.silu(causal_depthwise_conv(gdn_qkv, conv_w[l]))
        y_gdn = gated_deltanet(gdn_qkv, gdn_z, gdn_a, gdn_b, A_log[l], dt_bias[l], gdn_norm[l])
        y_att = chunk_attention(att_q, att_k, att_v, rel_bias[l])
        x = x + jnp.concatenate([y_gdn, y_att], axis=-1) @ w_out_mix[l]
        x = x + 0.5 * swiglu_ffn(rmsnorm(x, ffn2_norm[l]), ffn2_w_in[l], ffn2_w_out[l])
    return rmsnorm(x, final_norm)
```

```python
import functools

import jax
import jax.numpy as jnp
from jax import lax
from jax.experimental import pallas as pl
from jax.experimental.pallas import tpu as pltpu

F32 = jnp.float32
BF16 = jnp.bfloat16

D_MODEL = 1024
HEAD_DIM = 64
N_HEADS = 8
D_GROUP = N_HEADS * HEAD_DIM
D_FF = 2816
CONV_WIDTH = 4
ATT_CHUNK = 64
BAND_CHUNKS = 8
REL_CLIP = 128
RMS_EPS = 1e-6
L2_EPS = 1e-6
NEG_INF = -1e30

LANES = 128
N_PAIRS = D_GROUP // LANES
MXU_COLS = 256
FF_TILE = MXU_COLS
N_FF_TILES = D_FF // FF_TILE
ROW_TILE = 512
GDN_BLOCK = 128
ATT_TQ = 128
ATT_WIN = BAND_CHUNKS * ATT_CHUNK + ATT_TQ
CONV_HALO = 8
VMEM_LIMIT = 56 * 1024 * 1024


def _dot(a, b):
    return jnp.dot(a, b, preferred_element_type=F32)


def _dot_nt(a, b):
    return lax.dot_general(a, b, (((1,), (1,)), ((), ())), preferred_element_type=F32)


def _dot_tn(a, b):
    return lax.dot_general(a, b, (((0,), (0,)), ((), ())), preferred_element_type=F32)


def _dot_exact_lhs(x, sel):
    hi = x.astype(BF16)
    r1 = x - hi.astype(F32)
    mid = r1.astype(BF16)
    lo = (r1 - mid.astype(F32)).astype(BF16)
    return _dot(hi, sel) + _dot(mid, sel) + _dot(lo, sel)


def _dot_exact_rhs(sel, x):
    hi = x.astype(BF16)
    r1 = x - hi.astype(F32)
    mid = r1.astype(BF16)
    lo = (r1 - mid.astype(F32)).astype(BF16)
    return _dot(sel, hi) + _dot(sel, mid) + _dot(sel, lo)


def _rmsnorm(x, gain):
    ms = jnp.mean(x * x, axis=-1, keepdims=True)
    return x * lax.rsqrt(ms + RMS_EPS) * gain


def _silu(x):
    return x * (1.0 / (1.0 + jnp.exp(-x)))


def _ffn_half_step(x, gain_ref, win_ref, wout_ref, act_ref):
    h = _rmsnorm(x, gain_ref[...]).astype(BF16)
    for j in range(N_FF_TILES):
        lo = j * FF_TILE
        gate = _dot(h, win_ref[:, lo:lo + FF_TILE])
        up = _dot(h, win_ref[:, D_FF + lo:D_FF + lo + FF_TILE])
        act_ref[:, lo:lo + FF_TILE] = (_silu(gate) * up).astype(BF16)
    return x + 0.5 * _dot(act_ref[...], wout_ref[...])


def _ffn1_proj_kernel(x_ref, g1_ref, win_ref, wout_ref, gm_ref, wmix_ref, wab_ref,
                      x1_ref, gdn_ref, ab_ref, att_ref, act_ref):
    x1 = _ffn_half_step(x_ref[...], g1_ref, win_ref, wout_ref, act_ref)
    x1_ref[...] = x1
    h = _rmsnorm(x1, gm_ref[...]).astype(BF16)
    for t in range(4):
        lo = t * D_GROUP
        gdn_ref[:, lo:lo + D_GROUP] = _dot(h, wmix_ref[:, lo:lo + D_GROUP]).astype(BF16)
    ab_ref[...] = _dot(h, wab_ref[...])
    for t in range(3):
        lo = t * D_GROUP
        r = _dot(h, wmix_ref[:, 4 * D_GROUP + lo:4 * D_GROUP + lo + D_GROUP])
        if t == 0:
            r = r * (HEAD_DIM ** -0.5)
        att_ref[:, lo:lo + D_GROUP] = r.astype(BF16)


def _out_ffn2_kernel(x1_ref, yg_ref, ya_ref, wo_ref, g2_ref, win_ref, wout_ref, gf_ref,
                     o_ref, act_ref):
    x2 = (x1_ref[...] + _dot(yg_ref[...], wo_ref[0:D_GROUP, :])
          + _dot(ya_ref[...], wo_ref[D_GROUP:2 * D_GROUP, :]))
    x3 = _ffn_half_step(x2, g2_ref, win_ref, wout_ref, act_ref)
    o_ref[...] = _rmsnorm(x3, gf_ref[...])


def _attn_kernel(q_ref, k_ref, v_ref, bias_ref, o_ref, kwin_ref, vwin_ref):
    i = pl.program_id(1)
    n_pad_tiles = (ATT_WIN - ATT_TQ) // ATT_TQ

    @pl.when(i >= n_pad_tiles)
    def _():
        start = pl.multiple_of(i * ATT_TQ - (ATT_WIN - ATT_TQ), ATT_TQ)
        kwin_ref[...] = k_ref[pl.ds(start, ATT_WIN), :]
        vwin_ref[...] = v_ref[pl.ds(start, ATT_WIN), :]

    for c in range(n_pad_tiles):
        @pl.when(i == c)
        def _(c=c):
            n_pad = ATT_WIN - ATT_TQ - c * ATT_TQ
            zeros = jnp.zeros((n_pad, D_GROUP), BF16)
            kwin_ref[0:n_pad, :] = zeros
            vwin_ref[0:n_pad, :] = zeros
            kwin_ref[n_pad:ATT_WIN, :] = k_ref[0:ATT_WIN - n_pad, :]
            vwin_ref[n_pad:ATT_WIN, :] = v_ref[0:ATT_WIN - n_pad, :]

    key_pos = i * ATT_TQ - (ATT_WIN - ATT_TQ) + lax.broadcasted_iota(jnp.int32, (1, ATT_WIN), 1)
    pen = jnp.where(key_pos >= 0, 0.0, NEG_INF).astype(F32)
    first_head = lax.broadcasted_iota(jnp.int32, (1, LANES), 1) < HEAD_DIM

    for p in range(N_PAIRS):
        lo = p * LANES
        q2 = q_ref[:, lo:lo + LANES]
        zero = jnp.zeros_like(q2)
        qs = jnp.concatenate([jnp.where(first_head, q2, zero), jnp.where(first_head, zero, q2)], axis=0)
        s = _dot_nt(qs, kwin_ref[:, lo:lo + LANES]) + bias_ref[p] + pen
        m = jnp.max(s, axis=-1, keepdims=True)
        e = jnp.exp(s - m)
        denom = jnp.sum(e, axis=-1, keepdims=True)
        pv = _dot(e.astype(BF16), vwin_ref[:, lo:lo + LANES]) * (1.0 / denom)
        o_ref[:, lo:lo + LANES] = jnp.where(first_head, pv[0:ATT_TQ], pv[ATT_TQ:2 * ATT_TQ]).astype(BF16)


def _block_diag2(a, b):
    z = jnp.zeros_like(a)
    return jnp.concatenate([jnp.concatenate([a, z], axis=1), jnp.concatenate([z, b], axis=1)], axis=0)


def _unit_lower_inverse_pair(l0, l1):
    n = l0.shape[0]
    row = lax.broadcasted_iota(jnp.int32, (n, n), 0)
    col = lax.broadcasted_iota(jnp.int32, (n, n), 1)
    eye = (row == col).astype(F32)
    base = (row // 2 == col // 2)
    x0 = eye - jnp.where(base, l0, 0.0)
    x1 = eye - jnp.where(base, l1, 0.0)
    b = 2
    while b < n:
        cmask = (row // (2 * b) == col // (2 * b)) & (row % (2 * b) >= b) & (col % (2 * b) < b)
        c = jnp.concatenate([jnp.where(cmask, l0, 0.0), jnp.where(cmask, l1, 0.0)], axis=1).astype(BF16)
        xb0, xb1 = x0.astype(BF16), x1.astype(BF16)
        y = _dot(c, _block_diag2(xb0, xb1))
        yb = y.astype(BF16)
        z = _dot(jnp.concatenate([xb0, xb1], axis=1), _block_diag2(yb[:, 0:n], yb[:, n:2 * n]))
        x0 = x0 - z[:, 0:n]
        x1 = x1 - z[:, n:2 * n]
        b *= 2
    return x0, x1


def _gdn_kernel(qkvz_ref, ab_ref, convw_ref, arow_ref, dtrow_ref, gnorm_ref,
                tri_ref, ecol_ref, ea_ref, eb_ref, ones_ref,
                y_ref, xs_ref, state_ref):
    blk = pl.program_id(1)
    n = GDN_BLOCK
    d3 = 3 * D_GROUP

    @pl.when(blk == 0)
    def _():
        xs_ref[0:CONV_HALO, :] = jnp.zeros((CONV_HALO, d3), F32)
        state_ref[...] = jnp.zeros_like(state_ref)

    xs_ref[CONV_HALO:CONV_HALO + n, :] = qkvz_ref[:, 0:d3].astype(F32)
    conv = jnp.zeros((n, d3), F32)
    for j in range(CONV_WIDTH):
        off = CONV_HALO - (CONV_WIDTH - 1) + j
        conv = conv + convw_ref[j:j + 1, :] * xs_ref[off:off + n, :]
    xs_ref[0:CONV_HALO, :] = xs_ref[n:n + CONV_HALO, :]
    qkv = _silu(conv)

    ab = ab_ref[...]
    g_n = -arow_ref[...] * jax.nn.softplus(ab + dtrow_ref[...])
    beta_n = 1.0 / (1.0 + jnp.exp(-ab))
    gc_n = _dot_exact_rhs(tri_ref[...], g_n)
    gc_t = gc_n.T
    gc_col = _dot_exact_lhs(gc_n, ecol_ref[...])
    gc_x = _dot_exact_lhs(gc_n, ea_ref[...])
    beta_x = _dot_exact_lhs(beta_n, eb_ref[...])
    g_last = gc_x[n - 1:n, :]
    exp_gc = jnp.exp(gc_x)
    exp_rest = jnp.exp(g_last - gc_x)
    exp_last = jnp.exp(g_last)

    row = lax.broadcasted_iota(jnp.int32, (n, n), 0)
    col = lax.broadcasted_iota(jnp.int32, (n, n), 1)
    lower = row >= col
    strict = row > col
    lane = lax.broadcasted_iota(jnp.int32, (1, LANES), 1)
    first_head = lane < HEAD_DIM
    head_block = (lax.broadcasted_iota(jnp.int32, (LANES, LANES), 0) // HEAD_DIM
                  == lax.broadcasted_iota(jnp.int32, (LANES, LANES), 1) // HEAD_DIM)
    ones_bd = ones_ref[...]

    for p in range(N_PAIRS):
        lo = p * LANES
        q = qkv[:, lo:lo + LANES]
        k = qkv[:, D_GROUP + lo:D_GROUP + lo + LANES]
        v = qkv[:, 2 * D_GROUP + lo:2 * D_GROUP + lo + LANES]
        sq = _dot(jnp.concatenate([q * q, k * k], axis=0).astype(BF16), ones_bd)
        q = q * lax.rsqrt(sq[0:n] + L2_EPS) * (HEAD_DIM ** -0.5)
        k = k * lax.rsqrt(sq[n:2 * n] + L2_EPS)
        beta = beta_x[:, lo:lo + LANES]
        kb = k * beta
        vb = v * beta
        kw = kb * exp_gc[:, lo:lo + LANES]
        q_dec = q * exp_gc[:, lo:lo + LANES]
        k_rest = k * exp_rest[:, lo:lo + LANES]

        kbb, qb, kbf = kb.astype(BF16), q.astype(BF16), k.astype(BF16)
        zb = jnp.zeros_like(kbb)
        lhs = jnp.concatenate([jnp.where(first_head, kbb, zb), jnp.where(first_head, zb, kbb),
                               jnp.where(first_head, qb, zb), jnp.where(first_head, zb, qb)], axis=0)
        tt = _dot_nt(lhs, kbf)
        decs = []
        for hh in range(2):
            h = 2 * p + hh
            diff = gc_col[:, h * LANES:(h + 1) * LANES] - gc_t[h:h + 1, :]
            decs.append(jnp.where(lower, jnp.exp(jnp.where(lower, diff, 0.0)), 0.0))
        l0 = jnp.where(strict, tt[0:n] * decs[0], 0.0)
        l1 = jnp.where(strict, tt[n:2 * n] * decs[1], 0.0)
        a0 = tt[2 * n:3 * n] * decs[0]
        a1 = tt[3 * n:4 * n] * decs[1]
        t0, t1 = _unit_lower_inverse_pair(l0, l1)

        vbb, kwb = vb.astype(BF16), kw.astype(BF16)
        rhs = jnp.concatenate([
            jnp.concatenate([jnp.where(first_head, vbb, zb), jnp.where(first_head, kwb, zb)], axis=1),
            jnp.concatenate([jnp.where(first_head, zb, vbb), jnp.where(first_head, zb, kwb)], axis=1)], axis=0)
        uw = _dot(jnp.concatenate([t0, t1], axis=1).astype(BF16), rhs)
        u = uw[:, 0:LANES]
        w = uw[:, LANES:2 * LANES]

        state = state_ref[p]
        ws = _dot(jnp.concatenate([w, q_dec], axis=0).astype(BF16), state.astype(BF16))
        v_new = u - ws[0:n]
        vnb = v_new.astype(BF16)
        znb = jnp.zeros_like(vnb)
        intra = _dot(jnp.concatenate([a0, a1], axis=1).astype(BF16),
                     jnp.concatenate([jnp.where(first_head, vnb, znb), jnp.where(first_head, znb, vnb)], axis=0))
        o = ws[n:2 * n] + intra
        upd = _dot_tn(k_rest.astype(BF16), vnb)
        state_ref[p] = state * exp_last[:, lo:lo + LANES] + jnp.where(head_block, upd, 0.0)

        ms = _dot((o * o).astype(BF16), ones_bd) * (1.0 / HEAD_DIM)
        z = qkvz_ref[:, d3 + lo:d3 + lo + LANES].astype(F32)
        y = o * lax.rsqrt(ms + RMS_EPS) * gnorm_ref[:, lo:lo + LANES] * _silu(z)
        y_ref[:, lo:lo + LANES] = y.astype(BF16)


def _resident(shape):
    zeros = (0,) * len(shape)
    return pl.BlockSpec(shape, lambda *_: zeros, pipeline_mode=pl.Buffered(1))


def _attention_bias(rel_bias):
    qpos = jnp.arange(ATT_TQ)[:, None]
    kpos = jnp.arange(ATT_WIN)[None, :] - (ATT_WIN - ATT_TQ)
    idx = jnp.clip(qpos - kpos, -REL_CLIP, REL_CLIP) + REL_CLIP
    dchunk = qpos // ATT_CHUNK - kpos // ATT_CHUNK
    valid = (dchunk >= 0) & (dchunk <= BAND_CHUNKS)
    bias = jnp.where(valid[None], rel_bias.astype(F32)[:, idx], NEG_INF)
    return bias.reshape(N_PAIRS, 2 * ATT_TQ, ATT_WIN)


def _layer(x, ffn1_norm, ffn1_w_in, ffn1_w_out, mix_norm, w_in_mix, conv_w, A_log, dt_bias,
           gdn_norm, rel_bias, w_out_mix, ffn2_norm, ffn2_w_in, ffn2_w_out, final_norm, apply_final):
    bsz, seq, _ = x.shape
    n_tok = bsz * seq
    assert n_tok % ROW_TILE == 0 and seq % GDN_BLOCK == 0 and seq % ATT_TQ == 0 and seq >= ATT_WIN
    n_row_tiles = n_tok // ROW_TILE
    row = lambda i: (i, 0)
    params = functools.partial(pltpu.CompilerParams, vmem_limit_bytes=VMEM_LIMIT)

    gq = 3 * D_GROUP
    w_main = jnp.concatenate([w_in_mix[:, 0:4 * D_GROUP], w_in_mix[:, 4 * D_GROUP + 2 * N_HEADS:]], axis=1)
    w_ab = jnp.pad(w_in_mix[:, 4 * D_GROUP:4 * D_GROUP + 2 * N_HEADS], ((0, 0), (0, LANES - 2 * N_HEADS)))

    x1, gdn_in, ab, att_in = pl.pallas_call(
        _ffn1_proj_kernel,
        grid=(n_row_tiles,),
        in_specs=[pl.BlockSpec((ROW_TILE, D_MODEL), row),
                  _resident((1, D_MODEL)), _resident((D_MODEL, 2 * D_FF)), _resident((D_FF, D_MODEL)),
                  _resident((1, D_MODEL)), _resident((D_MODEL, 7 * D_GROUP)), _resident((D_MODEL, LANES))],
        out_specs=[pl.BlockSpec((ROW_TILE, D_MODEL), row), pl.BlockSpec((ROW_TILE, 4 * D_GROUP), row),
                   pl.BlockSpec((ROW_TILE, LANES), row), pl.BlockSpec((ROW_TILE, gq), row)],
        out_shape=[jax.ShapeDtypeStruct((n_tok, D_MODEL), F32), jax.ShapeDtypeStruct((n_tok, 4 * D_GROUP), BF16),
                   jax.ShapeDtypeStruct((n_tok, LANES), F32), jax.ShapeDtypeStruct((n_tok, gq), BF16)],
        scratch_shapes=[pltpu.VMEM((ROW_TILE, D_FF), BF16)],
        compiler_params=params(dimension_semantics=("arbitrary",)),
        name="ffn1_proj",
    )(x.reshape(n_tok, D_MODEL), ffn1_norm.reshape(1, D_MODEL).astype(F32),
      ffn1_w_in.astype(BF16), ffn1_w_out.astype(BF16),
      mix_norm.reshape(1, D_MODEL).astype(F32), w_main.astype(BF16), w_ab.astype(BF16))

    n = GDN_BLOCK
    head_of_lane = jnp.arange(D_GROUP) // HEAD_DIM
    lane16 = jnp.arange(LANES)
    tri = (jnp.arange(n)[:, None] >= jnp.arange(n)[None, :]).astype(BF16)
    e_col = (lane16[:, None] == (jnp.arange(N_HEADS * LANES) // LANES)[None, :]).astype(BF16)
    e_a = (lane16[:, None] == head_of_lane[None, :]).astype(BF16)
    e_b = (lane16[:, None] == (head_of_lane + N_HEADS)[None, :]).astype(BF16)
    ones_bd = (lane16[:, None] // HEAD_DIM == lane16[None, :] // HEAD_DIM).astype(BF16)
    a_row = jnp.pad(jnp.exp(A_log.astype(F32)), (0, LANES - N_HEADS)).reshape(1, LANES)
    dt_row = jnp.pad(dt_bias.astype(F32), (0, LANES - N_HEADS)).reshape(1, LANES)
    gnorm_row = jnp.tile(gdn_norm.astype(F32), N_HEADS).reshape(1, D_GROUP)
    blk3 = lambda b, i: (b, i, 0)

    y_gdn = pl.pallas_call(
        _gdn_kernel,
        grid=(bsz, seq // n),
        in_specs=[pl.BlockSpec((None, n, 4 * D_GROUP), blk3), pl.BlockSpec((None, n, LANES), blk3),
                  _resident((CONV_WIDTH, gq)), _resident((1, LANES)), _resident((1, LANES)),
                  _resident((1, D_GROUP)), _resident((n, n)), _resident((LANES, N_HEADS * LANES)),
                  _resident((LANES, D_GROUP)), _resident((LANES, D_GROUP)), _resident((LANES, LANES))],
        out_specs=pl.BlockSpec((None, n, D_GROUP), blk3),
        out_shape=jax.ShapeDtypeStruct((bsz, seq, D_GROUP), BF16),
        scratch_shapes=[pltpu.VMEM((n + CONV_HALO, gq), F32), pltpu.VMEM((N_PAIRS, LANES, LANES), F32)],
        compiler_params=params(dimension_semantics=("arbitrary", "arbitrary")),
        name="gated_deltanet",
    )(gdn_in.reshape(bsz, seq, 4 * D_GROUP), ab.reshape(bsz, seq, LANES), conv_w.astype(F32),
      a_row, dt_row, gnorm_row, tri, e_col, e_a, e_b, ones_bd)

    att3 = att_in.reshape(bsz, seq, gq)
    y_att = pl.pallas_call(
        _attn_kernel,
        grid=(bsz, seq // ATT_TQ),
        in_specs=[pl.BlockSpec((None, ATT_TQ, D_GROUP), lambda b, i: (b, i, 0)),
                  pl.BlockSpec((None, seq, D_GROUP), lambda b, i: (b, 0, 1)),
                  pl.BlockSpec((None, seq, D_GROUP), lambda b, i: (b, 0, 2)),
                  _resident((N_PAIRS, 2 * ATT_TQ, ATT_WIN))],
        out_specs=pl.BlockSpec((None, ATT_TQ, D_GROUP), blk3),
        out_shape=jax.ShapeDtypeStruct((bsz, seq, D_GROUP), BF16),
        scratch_shapes=[pltpu.VMEM((ATT_WIN, D_GROUP), BF16), pltpu.VMEM((ATT_WIN, D_GROUP), BF16)],
        compiler_params=params(dimension_semantics=("arbitrary", "arbitrary")),
        name="chunk_attention",
    )(att3, att3, att3, _attention_bias(rel_bias))

    gain_f = final_norm if apply_final else jnp.ones((D_MODEL,), F32)
    out = pl.pallas_call(
        functools.partial(_out_ffn2_kernel),
        grid=(n_row_tiles,),
        in_specs=[pl.BlockSpec((ROW_TILE, D_MODEL), row), pl.BlockSpec((ROW_TILE, D_GROUP), row),
                  pl.BlockSpec((ROW_TILE, D_GROUP), row), _resident((2 * D_GROUP, D_MODEL)),
                  _resident((1, D_MODEL)), _resident((D_MODEL, 2 * D_FF)), _resident((D_FF, D_MODEL)),
                  _resident((1, D_MODEL))],
        out_specs=pl.BlockSpec((ROW_TILE, D_MODEL), row),
        out_shape=jax.ShapeDtypeStruct((n_tok, D_MODEL), F32),
        scratch_shapes=[pltpu.VMEM((ROW_TILE, D_FF), BF16)],
        compiler_params=params(dimension_semantics=("arbitrary",)),
        name="out_ffn2_norm",
    )(x1, y_gdn.reshape(n_tok, D_GROUP), y_att.reshape(n_tok, D_GROUP), w_out_mix.astype(BF16),
      ffn2_norm.reshape(1, D_MODEL).astype(F32), ffn2_w_in.astype(BF16), ffn2_w_out.astype(BF16),
      gain_f.reshape(1, D_MODEL).astype(F32))
    return out.reshape(bsz, seq, D_MODEL)


def kernel(x, ffn1_norm, ffn1_w_in, ffn1_w_out, mix_norm, w_in_mix, conv_w, A_log, dt_bias, gdn_norm,
           rel_bias, w_out_mix, ffn2_norm, ffn2_w_in, ffn2_w_out, final_norm):
    depth = ffn1_norm.shape[0]
    assert depth == 1, "the fused final norm assumes a single layer"
    return _layer(x, ffn1_norm[0], ffn1_w_in[0], ffn1_w_out[0], mix_norm[0], w_in_mix[0], conv_w[0],
                  A_log[0], dt_bias[0], gdn_norm[0], rel_bias[0], w_out_mix[0], ffn2_norm[0],
                  ffn2_w_in[0], ffn2_w_out[0], final_norm, apply_final=True)
```

```python
import functools

import jax
import jax.numpy as jnp
from jax import lax
from jax.experimental import pallas as pl
from jax.experimental.pallas import tpu as pltpu

F32 = jnp.float32
BF16 = jnp.bfloat16

D_MODEL = 1024
HEAD_DIM = 64
N_HEADS = 8
D_GROUP = N_HEADS * HEAD_DIM
D_FF = 2816
CONV_WIDTH = 4
ATT_CHUNK = 64
BAND_CHUNKS = 8
REL_CLIP = 128
RMS_EPS = 1e-6
L2_EPS = 1e-6
NEG_INF = -1e30

LANES = 128
SUBLANES = 8
N_PAIRS = D_GROUP // LANES
MXU_COLS = 256
FF_TILE = MXU_COLS
N_FF_TILES = D_FF // FF_TILE
ROW_TILE = 512
GDN_BLOCK = 128
GDN_STEP_BLOCKS = 2
ATT_TQ = 128
ATT_WIN = BAND_CHUNKS * ATT_CHUNK + ATT_TQ
CONV_HALO = SUBLANES
VMEM_LIMIT = 56 * 1024 * 1024


def _dot(a, b):
    return jnp.dot(a, b, preferred_element_type=F32)


def _dot_nt(a, b):
    return lax.dot_general(a, b, (((1,), (1,)), ((), ())), preferred_element_type=F32)


def _dot_tn(a, b):
    return lax.dot_general(a, b, (((0,), (0,)), ((), ())), preferred_element_type=F32)


def _bf16_pieces(x, n_pieces):
    pieces = []
    for _ in range(n_pieces - 1):
        hi = x.astype(BF16)
        pieces.append(hi)
        x = x - hi.astype(F32)
    pieces.append(x.astype(BF16))
    return pieces


def _select_cols(x, sel, n_pieces=3):
    return functools.reduce(lambda a, b: a + b, [_dot(p, sel) for p in _bf16_pieces(x, n_pieces)])


def _select_rows(sel, x, n_pieces=3):
    return functools.reduce(lambda a, b: a + b, [_dot(sel, p) for p in _bf16_pieces(x, n_pieces)])


def _rmsnorm(x, gain):
    ms = jnp.mean(x * x, axis=-1, keepdims=True)
    return x * lax.rsqrt(ms + RMS_EPS) * gain


def _silu(x):
    return x * (1.0 / (1.0 + jnp.exp(-x)))


def _softplus(x):
    return jnp.maximum(x, 0.0) + jnp.log(1.0 + jnp.exp(-jnp.abs(x)))


def _ffn_half_step(x, gain_ref, win_ref, wout_ref, act_ref):
    h = _rmsnorm(x, gain_ref[...]).astype(BF16)
    for j in range(N_FF_TILES):
        lo = j * FF_TILE
        gate = _dot(h, win_ref[:, lo:lo + FF_TILE])
        up = _dot(h, win_ref[:, D_FF + lo:D_FF + lo + FF_TILE])
        act_ref[:, lo:lo + FF_TILE] = (_silu(gate) * up).astype(BF16)
    return x + 0.5 * _dot(act_ref[...], wout_ref[...])


def _ffn1_proj_kernel(tiles_per_seq, x_ref, g1_ref, win_ref, wout_ref, gm_ref, wmix_ref, wab_ref, convw_ref,
                      x1_ref, gdn_ref, ab_ref, att_ref, act_ref, conv_ref):
    rows = x_ref.shape[0]
    x1 = _ffn_half_step(x_ref[...], g1_ref, win_ref, wout_ref, act_ref)
    x1_ref[...] = x1
    h = _rmsnorm(x1, gm_ref[...]).astype(BF16)

    @pl.when(pl.program_id(0) % tiles_per_seq == 0)
    def _():
        conv_ref[0:CONV_HALO, :] = jnp.zeros((CONV_HALO, 3 * D_GROUP), F32)

    for t in range(3):
        lo = t * D_GROUP
        conv_ref[CONV_HALO:CONV_HALO + rows, lo:lo + D_GROUP] = _dot(h, wmix_ref[:, lo:lo + D_GROUP])
        acc = None
        for j in range(CONV_WIDTH):
            off = CONV_HALO - (CONV_WIDTH - 1) + j
            term = convw_ref[j:j + 1, lo:lo + D_GROUP] * conv_ref[off:off + rows, lo:lo + D_GROUP]
            acc = term if acc is None else acc + term
        gdn_ref[:, lo:lo + D_GROUP] = _silu(acc).astype(BF16)
    conv_ref[0:CONV_HALO, :] = conv_ref[rows:rows + CONV_HALO, :]

    lo = 3 * D_GROUP
    gdn_ref[:, lo:lo + D_GROUP] = _dot(h, wmix_ref[:, lo:lo + D_GROUP]).astype(BF16)
    ab_ref[...] = _dot(h, wab_ref[...])
    for t in range(3):
        lo = t * D_GROUP
        r = _dot(h, wmix_ref[:, 4 * D_GROUP + lo:4 * D_GROUP + lo + D_GROUP])
        if t == 0:
            r = r * (HEAD_DIM ** -0.5)
        att_ref[:, lo:lo + D_GROUP] = r.astype(BF16)


def _out_ffn2_kernel(x1_ref, yg_ref, ya_ref, wo_ref, g2_ref, win_ref, wout_ref, gf_ref,
                     o_ref, act_ref):
    x2 = (x1_ref[...] + _dot(yg_ref[...], wo_ref[0:D_GROUP, :])
          + _dot(ya_ref[...], wo_ref[D_GROUP:2 * D_GROUP, :]))
    x3 = _ffn_half_step(x2, g2_ref, win_ref, wout_ref, act_ref)
    o_ref[...] = _rmsnorm(x3, gf_ref[...])


def _attn_kernel(q_ref, k_ref, v_ref, bias_ref, o_ref, kwin_ref, vwin_ref):
    i = pl.program_id(1)
    n_pad_tiles = (ATT_WIN - ATT_TQ) // ATT_TQ

    @pl.when(i >= n_pad_tiles)
    def _():
        start = pl.multiple_of(i * ATT_TQ - (ATT_WIN - ATT_TQ), ATT_TQ)
        kwin_ref[...] = k_ref[pl.ds(start, ATT_WIN), :]
        vwin_ref[...] = v_ref[pl.ds(start, ATT_WIN), :]

    for c in range(n_pad_tiles):
        @pl.when(i == c)
        def _(c=c):
            n_pad = ATT_WIN - ATT_TQ - c * ATT_TQ
            zeros = jnp.zeros((n_pad, D_GROUP), BF16)
            kwin_ref[0:n_pad, :] = zeros
            vwin_ref[0:n_pad, :] = zeros
            kwin_ref[n_pad:ATT_WIN, :] = k_ref[0:ATT_WIN - n_pad, :]
            vwin_ref[n_pad:ATT_WIN, :] = v_ref[0:ATT_WIN - n_pad, :]

    key_pos = i * ATT_TQ - (ATT_WIN - ATT_TQ) + lax.broadcasted_iota(jnp.int32, (1, ATT_WIN), 1)
    pen = jnp.where(key_pos >= 0, 0.0, NEG_INF).astype(F32)
    first_head = lax.broadcasted_iota(jnp.int32, (1, LANES), 1) < HEAD_DIM

    for p in range(N_PAIRS):
        lo = p * LANES
        q2 = q_ref[:, lo:lo + LANES]
        zero = jnp.zeros_like(q2)
        qs = jnp.concatenate([jnp.where(first_head, q2, zero), jnp.where(first_head, zero, q2)], axis=0)
        s = _dot_nt(qs, kwin_ref[:, lo:lo + LANES]) + bias_ref[p] + pen
        m = jnp.max(s, axis=-1, keepdims=True)
        e = jnp.exp(s - m)
        denom = jnp.sum(e, axis=-1, keepdims=True)
        pv = _dot(e.astype(BF16), vwin_ref[:, lo:lo + LANES]) * (1.0 / denom)
        o_ref[:, lo:lo + LANES] = jnp.where(first_head, pv[0:ATT_TQ], pv[ATT_TQ:2 * ATT_TQ]).astype(BF16)


def _block_diag2(a, b):
    z = jnp.zeros_like(a)
    return jnp.concatenate([jnp.concatenate([a, z], axis=1), jnp.concatenate([z, b], axis=1)], axis=0)


def _by_head(first_head, x):
    z = jnp.zeros_like(x)
    return jnp.where(first_head, x, z), jnp.where(first_head, z, x)


def _unit_lower_inverses(l_pairs):
    n = l_pairs[0][0].shape[0]
    row = lax.broadcasted_iota(jnp.int32, (n, n), 0)
    col = lax.broadcasted_iota(jnp.int32, (n, n), 1)
    eye = (row == col).astype(F32)
    base = (row // 2 == col // 2)
    xs = [(eye - jnp.where(base, l0, 0.0), eye - jnp.where(base, l1, 0.0)) for l0, l1 in l_pairs]
    lbs = [(l0.astype(BF16), l1.astype(BF16)) for l0, l1 in l_pairs]
    b = 2
    while b < n:
        cmask = ((row // (2 * b) == col // (2 * b)) & (row % (2 * b) >= b) & (col % (2 * b) < b)).astype(BF16)
        xbs = [(x0.astype(BF16), x1.astype(BF16)) for x0, x1 in xs]
        ys = [_dot(jnp.concatenate([lb0 * cmask, lb1 * cmask], axis=1), _block_diag2(xb0, xb1)).astype(BF16)
              for (lb0, lb1), (xb0, xb1) in zip(lbs, xbs)]
        zs = [_dot(jnp.concatenate([xb0, xb1], axis=1), _block_diag2(y[:, 0:n], y[:, n:2 * n]))
              for (xb0, xb1), y in zip(xbs, ys)]
        xs = [(x0 - z[:, 0:n], x1 - z[:, n:2 * n]) for (x0, x1), z in zip(xs, zs)]
        b *= 2
    return xs


def _gdn_kernel(qkvz_ref, ab_ref, arow_ref, dtrow_ref, gnorm_ref, tri_ref, ea_ref, eb_ref, ones_ref,
                y_ref, state_ref):
    n = GDN_BLOCK
    d3 = 3 * D_GROUP

    @pl.when(pl.program_id(1) == 0)
    def _():
        state_ref[...] = jnp.zeros_like(state_ref)

    row = lax.broadcasted_iota(jnp.int32, (n, n), 0)
    col = lax.broadcasted_iota(jnp.int32, (n, n), 1)
    lower = row >= col
    strict = row > col
    first_head = lax.broadcasted_iota(jnp.int32, (1, LANES), 1) < HEAD_DIM
    head_block = (lax.broadcasted_iota(jnp.int32, (LANES, LANES), 0) // HEAD_DIM
                  == lax.broadcasted_iota(jnp.int32, (LANES, LANES), 1) // HEAD_DIM)
    ones_bd = ones_ref[...]

    blocks = []
    for b in range(GDN_STEP_BLOCKS):
        r0 = b * n
        ab = ab_ref[r0:r0 + n, :]
        g_n = -arow_ref[...] * _softplus(ab + dtrow_ref[...])
        beta_n = 1.0 / (1.0 + jnp.exp(-ab))
        gc_n = _select_rows(tri_ref[...], g_n)
        gc_x = _select_cols(gc_n, ea_ref[...])
        g_last = gc_x[n - 1:n, :]
        blocks.append(dict(
            r0=r0, gc_t=gc_n.T, gc_x=gc_x, beta_x=_select_cols(beta_n, eb_ref[...], n_pieces=2),
            exp_gc=jnp.exp(gc_x), exp_rest=jnp.exp(g_last - gc_x), exp_last=jnp.exp(g_last)))

    chains = []
    for blk in blocks:
        for p in range(N_PAIRS):
            lo = p * LANES
            r0 = blk["r0"]
            q = qkvz_ref[r0:r0 + n, lo:lo + LANES].astype(F32)
            k = qkvz_ref[r0:r0 + n, D_GROUP + lo:D_GROUP + lo + LANES].astype(F32)
            v = qkvz_ref[r0:r0 + n, 2 * D_GROUP + lo:2 * D_GROUP + lo + LANES].astype(F32)
            chains.append(dict(blk=blk, p=p, lo=lo, q=q, k=k, v=v))

    for c in chains:
        c["sq"] = _dot(jnp.concatenate([c["q"] * c["q"], c["k"] * c["k"]], axis=0).astype(BF16), ones_bd)
    for c in chains:
        blk, lo = c["blk"], c["lo"]
        q = c["q"] * lax.rsqrt(c["sq"][0:n] + L2_EPS) * (HEAD_DIM ** -0.5)
        k = c["k"] * lax.rsqrt(c["sq"][n:2 * n] + L2_EPS)
        beta = blk["beta_x"][:, lo:lo + LANES]
        kb = k * beta
        c["vb"] = (c["v"] * beta).astype(BF16)
        c["kw"] = (kb * blk["exp_gc"][:, lo:lo + LANES]).astype(BF16)
        c["q_dec"] = (q * blk["exp_gc"][:, lo:lo + LANES]).astype(BF16)
        c["k_rest"] = (k * blk["exp_rest"][:, lo:lo + LANES]).astype(BF16)
        c["kb"], c["qb"], c["kf"] = kb.astype(BF16), q.astype(BF16), k.astype(BF16)
    for c in chains:
        lhs = jnp.concatenate(_by_head(first_head, c["kb"]) + _by_head(first_head, c["qb"]), axis=0)
        c["tt"] = _dot_nt(lhs, c["kf"])
    l_pairs = []
    for c in chains:
        blk, p, lo = c["blk"], c["p"], c["lo"]
        g2 = blk["gc_x"][:, lo:lo + LANES]
        g2_swapped = pltpu.roll(g2, HEAD_DIM, axis=1)
        cols = (jnp.where(first_head, g2, g2_swapped), jnp.where(first_head, g2_swapped, g2))
        tt = c["tt"]
        ls, aqk = [], []
        for hh in range(2):
            h = 2 * p + hh
            dec = jnp.exp(jnp.minimum(cols[hh] - blk["gc_t"][h:h + 1, :], 0.0))
            ls.append(jnp.where(strict, tt[hh * n:(hh + 1) * n] * dec, 0.0))
            aqk.append(jnp.where(lower, tt[(2 + hh) * n:(3 + hh) * n] * dec, 0.0).astype(BF16))
        l_pairs.append(tuple(ls))
        c["aqk"] = jnp.concatenate(aqk, axis=1)
    for c, (t0, t1) in zip(chains, _unit_lower_inverses(l_pairs)):
        vb0, vb1 = _by_head(first_head, c["vb"])
        kw0, kw1 = _by_head(first_head, c["kw"])
        rhs = jnp.concatenate([jnp.concatenate([vb0, kw0], axis=1), jnp.concatenate([vb1, kw1], axis=1)], axis=0)
        c["uw"] = _dot(jnp.concatenate([t0, t1], axis=1).astype(BF16), rhs)

    for b in range(GDN_STEP_BLOCKS):
        cs = chains[b * N_PAIRS:(b + 1) * N_PAIRS]
        for c in cs:
            c["state"] = state_ref[c["p"]]
            w = c["uw"][:, LANES:2 * LANES].astype(BF16)
            c["ws"] = _dot(jnp.concatenate([w, c["q_dec"]], axis=0), c["state"].astype(BF16))
        for c in cs:
            c["v_new"] = (c["uw"][:, 0:LANES] - c["ws"][0:n]).astype(BF16)
        for c in cs:
            c["upd"] = _dot_tn(c["k_rest"], c["v_new"])
            c["intra"] = _dot(c["aqk"], jnp.concatenate(_by_head(first_head, c["v_new"]), axis=0))
        for c in cs:
            lo = c["lo"]
            state_ref[c["p"]] = (c["state"] * c["blk"]["exp_last"][:, lo:lo + LANES]
                                 + jnp.where(head_block, c["upd"], 0.0))
            c["o"] = c["ws"][n:2 * n] + c["intra"]

    for c in chains:
        c["ms"] = _dot((c["o"] * c["o"]).astype(BF16), ones_bd) * (1.0 / HEAD_DIM)
    for c in chains:
        lo, r0 = c["lo"], c["blk"]["r0"]
        z = qkvz_ref[r0:r0 + n, d3 + lo:d3 + lo + LANES].astype(F32)
        y = c["o"] * lax.rsqrt(c["ms"] + RMS_EPS) * gnorm_ref[:, lo:lo + LANES] * _silu(z)
        y_ref[r0:r0 + n, lo:lo + LANES] = y.astype(BF16)


def _resident(shape):
    zeros = (0,) * len(shape)
    return pl.BlockSpec(shape, lambda *_: zeros, pipeline_mode=pl.Buffered(1))


def _attention_bias(rel_bias):
    span = ATT_WIN + ATT_TQ - 1
    rel = (ATT_WIN - 1) - jnp.arange(span)
    u = rel_bias.astype(F32)[:, jnp.clip(rel, -REL_CLIP, REL_CLIP) + REL_CLIP]
    u = jnp.pad(u, ((0, 0), (0, 1)))
    skew = jnp.tile(u, (1, ATT_TQ))[:, :ATT_TQ * span].reshape(N_HEADS, ATT_TQ, span)
    bias = skew[:, :, ATT_TQ - 1:ATT_TQ - 1 + ATT_WIN]
    qpos = jnp.arange(ATT_TQ)[:, None]
    kpos = jnp.arange(ATT_WIN)[None, :] - (ATT_WIN - ATT_TQ)
    dchunk = qpos // ATT_CHUNK - kpos // ATT_CHUNK
    valid = (dchunk >= 0) & (dchunk <= BAND_CHUNKS)
    return jnp.where(valid[None], bias, NEG_INF).reshape(N_PAIRS, 2 * ATT_TQ, ATT_WIN)


def _layer(x, ffn1_norm, ffn1_w_in, ffn1_w_out, mix_norm, w_in_mix, conv_w, A_log, dt_bias,
           gdn_norm, rel_bias, w_out_mix, ffn2_norm, ffn2_w_in, ffn2_w_out, final_norm):
    bsz, seq, _ = x.shape
    n_tok = bsz * seq
    gdn_rows = GDN_BLOCK * GDN_STEP_BLOCKS
    assert seq % ROW_TILE == 0 and seq % gdn_rows == 0 and seq % ATT_TQ == 0 and seq >= ATT_WIN
    n_row_tiles = n_tok // ROW_TILE
    row = lambda i: (i, 0)
    params = functools.partial(pltpu.CompilerParams, vmem_limit_bytes=VMEM_LIMIT)

    gq = 3 * D_GROUP
    w_main = jnp.concatenate([w_in_mix[:, 0:4 * D_GROUP], w_in_mix[:, 4 * D_GROUP + 2 * N_HEADS:]], axis=1)
    w_ab = jnp.pad(w_in_mix[:, 4 * D_GROUP:4 * D_GROUP + 2 * N_HEADS], ((0, 0), (0, LANES - 2 * N_HEADS)))

    x1, gdn_in, ab, att_in = pl.pallas_call(
        functools.partial(_ffn1_proj_kernel, seq // ROW_TILE),
        grid=(n_row_tiles,),
        in_specs=[pl.BlockSpec((ROW_TILE, D_MODEL), row),
                  _resident((1, D_MODEL)), _resident((D_MODEL, 2 * D_FF)), _resident((D_FF, D_MODEL)),
                  _resident((1, D_MODEL)), _resident((D_MODEL, 7 * D_GROUP)), _resident((D_MODEL, LANES)),
                  _resident((CONV_WIDTH, gq))],
        out_specs=[pl.BlockSpec((ROW_TILE, D_MODEL), row), pl.BlockSpec((ROW_TILE, 4 * D_GROUP), row),
                   pl.BlockSpec((ROW_TILE, LANES), row), pl.BlockSpec((ROW_TILE, gq), row)],
        out_shape=[jax.ShapeDtypeStruct((n_tok, D_MODEL), F32), jax.ShapeDtypeStruct((n_tok, 4 * D_GROUP), BF16),
                   jax.ShapeDtypeStruct((n_tok, LANES), F32), jax.ShapeDtypeStruct((n_tok, gq), BF16)],
        scratch_shapes=[pltpu.VMEM((ROW_TILE, D_FF), BF16), pltpu.VMEM((ROW_TILE + CONV_HALO, gq), F32)],
        compiler_params=params(dimension_semantics=("arbitrary",)),
        name="ffn1_proj",
    )(x.reshape(n_tok, D_MODEL), ffn1_norm.reshape(1, D_MODEL).astype(F32),
      ffn1_w_in.astype(BF16), ffn1_w_out.astype(BF16),
      mix_norm.reshape(1, D_MODEL).astype(F32), w_main.astype(BF16), w_ab.astype(BF16), conv_w.astype(F32))

    n = GDN_BLOCK
    head_of_lane = jnp.arange(D_GROUP) // HEAD_DIM
    lane = jnp.arange(LANES)
    tri = (jnp.arange(n)[:, None] >= jnp.arange(n)[None, :]).astype(BF16)
    e_a = (lane[:, None] == head_of_lane[None, :]).astype(BF16)
    e_b = (lane[:, None] == (head_of_lane + N_HEADS)[None, :]).astype(BF16)
    ones_bd = (lane[:, None] // HEAD_DIM == lane[None, :] // HEAD_DIM).astype(BF16)
    a_row = jnp.pad(jnp.exp(A_log.astype(F32)), (0, LANES - N_HEADS)).reshape(1, LANES)
    dt_row = jnp.pad(dt_bias.astype(F32), (0, LANES - N_HEADS)).reshape(1, LANES)
    gnorm_row = jnp.tile(gdn_norm.astype(F32), N_HEADS).reshape(1, D_GROUP)
    blk3 = lambda b, i: (b, i, 0)

    y_gdn = pl.pallas_call(
        _gdn_kernel,
        grid=(bsz, seq // gdn_rows),
        in_specs=[pl.BlockSpec((None, gdn_rows, 4 * D_GROUP), blk3), pl.BlockSpec((None, gdn_rows, LANES), blk3),
                  _resident((1, LANES)), _resident((1, LANES)), _resident((1, D_GROUP)), _resident((n, n)),
                  _resident((LANES, D_GROUP)), _resident((LANES, D_GROUP)), _resident((LANES, LANES))],
        out_specs=pl.BlockSpec((None, gdn_rows, D_GROUP), blk3),
        out_shape=jax.ShapeDtypeStruct((bsz, seq, D_GROUP), BF16),
        scratch_shapes=[pltpu.VMEM((N_PAIRS, LANES, LANES), F32)],
        compiler_params=params(dimension_semantics=("arbitrary", "arbitrary")),
        name="gated_deltanet",
    )(gdn_in.reshape(bsz, seq, 4 * D_GROUP), ab.reshape(bsz, seq, LANES),
      a_row, dt_row, gnorm_row, tri, e_a, e_b, ones_bd)

    att3 = att_in.reshape(bsz, seq, gq)
    y_att = pl.pallas_call(
        _attn_kernel,
        grid=(bsz, seq // ATT_TQ),
        in_specs=[pl.BlockSpec((None, ATT_TQ, D_GROUP), lambda b, i: (b, i, 0)),
                  pl.BlockSpec((None, seq, D_GROUP), lambda b, i: (b, 0, 1)),
                  pl.BlockSpec((None, seq, D_GROUP), lambda b, i: (b, 0, 2)),
                  _resident((N_PAIRS, 2 * ATT_TQ, ATT_WIN))],
        out_specs=pl.BlockSpec((None, ATT_TQ, D_GROUP), blk3),
        out_shape=jax.ShapeDtypeStruct((bsz, seq, D_GROUP), BF16),
        scratch_shapes=[pltpu.VMEM((ATT_WIN, D_GROUP), BF16), pltpu.VMEM((ATT_WIN, D_GROUP), BF16)],
        compiler_params=params(dimension_semantics=("arbitrary", "arbitrary")),
        name="chunk_attention",
    )(att3, att3, att3, _attention_bias(rel_bias))

    out = pl.pallas_call(
        _out_ffn2_kernel,
        grid=(n_row_tiles,),
        in_specs=[pl.BlockSpec((ROW_TILE, D_MODEL), row), pl.BlockSpec((ROW_TILE, D_GROUP), row),
                  pl.BlockSpec((ROW_TILE, D_GROUP), row), _resident((2 * D_GROUP, D_MODEL)),
                  _resident((1, D_MODEL)), _resident((D_MODEL, 2 * D_FF)), _resident((D_FF, D_MODEL)),
                  _resident((1, D_MODEL))],
        out_specs=pl.BlockSpec((ROW_TILE, D_MODEL), row),
        out_shape=jax.ShapeDtypeStruct((n_tok, D_MODEL), F32),
        scratch_shapes=[pltpu.VMEM((ROW_TILE, D_FF), BF16)],
        compiler_params=params(dimension_semantics=("arbitrary",)),
        name="out_ffn2_norm",
    )(x1, y_gdn.reshape(n_tok, D_GROUP), y_att.reshape(n_tok, D_GROUP), w_out_mix.astype(BF16),
      ffn2_norm.reshape(1, D_MODEL).astype(F32), ffn2_w_in.astype(BF16), ffn2_w_out.astype(BF16),
      final_norm.reshape(1, D_MODEL).astype(F32))
    return out.reshape(bsz, seq, D_MODEL)


def kernel(x, ffn1_norm, ffn1_w_in, ffn1_w_out, mix_norm, w_in_mix, conv_w, A_log, dt_bias, gdn_norm,
           rel_bias, w_out_mix, ffn2_norm, ffn2_w_in, ffn2_w_out, final_norm):
    assert ffn1_norm.shape[0] == 1, "the final norm is fused into the single layer's last call"
    return _layer(x, ffn1_norm[0], ffn1_w_in[0], ffn1_w_out[0], mix_norm[0], w_in_mix[0], conv_w[0],
                  A_log[0], dt_bias[0], gdn_norm[0], rel_bias[0], w_out_mix[0], ffn2_norm[0],
                  ffn2_w_in[0], ffn2_w_out[0], final_norm)
```

```python
import functools

import jax
import jax.numpy as jnp
from jax import lax
from jax.experimental import pallas as pl
from jax.experimental.pallas import tpu as pltpu

F32 = jnp.float32
BF16 = jnp.bfloat16

D_MODEL = 1024
HEAD_DIM = 64
N_HEADS = 8
D_GROUP = N_HEADS * HEAD_DIM
D_FF = 2816
CONV_WIDTH = 4
ATT_CHUNK = 64
BAND_CHUNKS = 8
REL_CLIP = 128
RMS_EPS = 1e-6
L2_EPS = 1e-6
NEG_INF = -1e30

LANES = 128
SUBLANES = 8
N_PAIRS = D_GROUP // LANES
MXU_COLS = 256
FF_TILE = MXU_COLS
N_FF_TILES = D_FF // FF_TILE
ROW_TILE = 512
OUT_ROW_TILE = 1024
GDN_BLOCK = 128
GDN_STEP_BLOCKS = 2
ATT_TQ = 128
ATT_BACK = BAND_CHUNKS * ATT_CHUNK
ATT_WIN = ATT_BACK + ATT_TQ
ATT_STEP_TILES = ATT_BACK // ATT_TQ
LOG2E = 1.4426950408889634


def _bias_free_tiles():
    free = []
    for w in range(ATT_WIN // LANES):
        k_lo, k_hi = w * LANES - ATT_BACK, w * LANES - ATT_BACK + LANES - 1
        clipped = 0 - k_hi >= REL_CLIP
        dchunks = [qc - kc for qc in range(ATT_TQ // ATT_CHUNK) for kc in (k_lo // ATT_CHUNK, k_hi // ATT_CHUNK)]
        if clipped and min(dchunks) >= 0 and max(dchunks) <= BAND_CHUNKS:
            free.append(w)
    return tuple(free)


ATT_BIAS_FREE_TILES = _bias_free_tiles()
CONV_HALO = SUBLANES
VMEM_LIMIT = 56 * 1024 * 1024


def _dot(a, b):
    return jnp.dot(a, b, preferred_element_type=F32)


def _dot_nt(a, b):
    return lax.dot_general(a, b, (((1,), (1,)), ((), ())), preferred_element_type=F32)


def _dot_tn(a, b):
    return lax.dot_general(a, b, (((0,), (0,)), ((), ())), preferred_element_type=F32)


def _bf16_pieces(x, n_pieces):
    pieces = []
    for _ in range(n_pieces - 1):
        hi = x.astype(BF16)
        pieces.append(hi)
        x = x - hi.astype(F32)
    pieces.append(x.astype(BF16))
    return pieces


def _select_cols(x, sel, n_pieces=3):
    return functools.reduce(lambda a, b: a + b, [_dot(p, sel) for p in _bf16_pieces(x, n_pieces)])


def _select_rows(sel, x, n_pieces=3):
    return functools.reduce(lambda a, b: a + b, [_dot(sel, p) for p in _bf16_pieces(x, n_pieces)])


def _rmsnorm(x, gain):
    ms = jnp.mean(x * x, axis=-1, keepdims=True)
    return x * lax.rsqrt(ms + RMS_EPS) * gain


def _silu(x):
    return x * (1.0 / (1.0 + jnp.exp(-x)))


def _softplus(x):
    return jnp.maximum(x, 0.0) + jnp.log(1.0 + jnp.exp(-jnp.abs(x)))


def _by_head(first_head, x):
    z = jnp.zeros_like(x)
    return jnp.where(first_head, x, z), jnp.where(first_head, z, x)


def _ffn_half_step(x, gain_ref, win_ref, wout_ref, act_ref):
    h = _rmsnorm(x, gain_ref[...]).astype(BF16)
    for j in range(N_FF_TILES):
        lo = j * FF_TILE
        gate = _dot(h, win_ref[:, lo:lo + FF_TILE])
        up = _dot(h, win_ref[:, D_FF + lo:D_FF + lo + FF_TILE])
        act_ref[:, lo:lo + FF_TILE] = (_silu(gate) * up).astype(BF16)
    return x + 0.5 * _dot(act_ref[...], wout_ref[...])


def _ffn1_proj_kernel(tiles_per_seq, x_ref, g1_ref, win_ref, wout_ref, gm_ref, wmix_ref, wab_ref, convw_ref,
                      x1_ref, qkv_ref, z_ref, ab_ref, att_ref, act_ref, pre_ref):
    i = pl.program_id(0)
    rows = x_ref.shape[0]
    d3 = 3 * D_GROUP

    @pl.when(i == 0)
    def _():
        pre_ref[...] = jnp.zeros_like(pre_ref)

    @pl.when((i + tiles_per_seq - 1) % tiles_per_seq == 0)
    def _():
        pre_ref[0:CONV_HALO, :] = jnp.zeros((CONV_HALO, d3), F32)

    for t in range(3):
        lo = t * D_GROUP
        acc = None
        for j in range(CONV_WIDTH):
            off = CONV_HALO - (CONV_WIDTH - 1) + j
            term = convw_ref[j:j + 1, lo:lo + D_GROUP] * pre_ref[off:off + rows, lo:lo + D_GROUP]
            acc = term if acc is None else acc + term
        qkv_ref[:, lo:lo + D_GROUP] = _silu(acc).astype(BF16)
    pre_ref[0:CONV_HALO, :] = pre_ref[rows:rows + CONV_HALO, :]

    x1 = _ffn_half_step(x_ref[...], g1_ref, win_ref, wout_ref, act_ref)
    x1_ref[...] = x1
    h = _rmsnorm(x1, gm_ref[...]).astype(BF16)
    for t in range(3):
        lo = t * D_GROUP
        pre_ref[CONV_HALO:CONV_HALO + rows, lo:lo + D_GROUP] = _dot(h, wmix_ref[:, lo:lo + D_GROUP])
    z_ref[...] = _dot(h, wmix_ref[:, d3:d3 + D_GROUP]).astype(BF16)
    ab_ref[...] = _dot(h, wab_ref[...])
    for t in range(3):
        lo = t * D_GROUP
        r = _dot(h, wmix_ref[:, 4 * D_GROUP + lo:4 * D_GROUP + lo + D_GROUP])
        if t == 0:
            r = r * (HEAD_DIM ** -0.5 * LOG2E)
        att_ref[:, lo:lo + D_GROUP] = r.astype(BF16)


def _out_ffn2_kernel(x1_ref, yg_ref, ya_ref, wo_ref, g2_ref, win_ref, wout_ref, gf_ref,
                     o_ref, act_ref):
    x2 = (x1_ref[...] + _dot(yg_ref[...], wo_ref[0:D_GROUP, :])
          + _dot(ya_ref[...], wo_ref[D_GROUP:2 * D_GROUP, :]))
    x3 = _ffn_half_step(x2, g2_ref, win_ref, wout_ref, act_ref)
    o_ref[...] = _rmsnorm(x3, gf_ref[...])


def _attention_tiles(q_ref, bias_ref, o_ref, windows):
    first_head = lax.broadcasted_iota(jnp.int32, (1, LANES), 1) < HEAD_DIM
    units = [(t, p) for t in range(len(windows)) for p in range(N_PAIRS)]

    def scores(t, p):
        load_k, _, n_keys = windows[t]
        lo = p * LANES
        q2 = q_ref[t * ATT_TQ:(t + 1) * ATT_TQ, lo:lo + LANES]
        qs = jnp.concatenate(_by_head(first_head, q2), axis=0)
        s = _dot_nt(qs, load_k(lo))
        cols = []
        for w in range((ATT_WIN - n_keys) // LANES, ATT_WIN // LANES):
            c0 = w * LANES - (ATT_WIN - n_keys)
            sw = s[:, c0:c0 + LANES]
            if w not in ATT_BIAS_FREE_TILES:
                sw = sw + bias_ref[p, :, w * LANES:(w + 1) * LANES]
            cols.append(sw)
        return jnp.concatenate(cols, axis=1)

    s_next = scores(*units[0])
    for idx, (t, p) in enumerate(units):
        s = s_next
        if idx + 1 < len(units):
            s_next = scores(*units[idx + 1])
        lo = p * LANES
        m = jnp.max(s, axis=-1, keepdims=True)
        e = jnp.exp2(s - m)
        denom = jnp.sum(e, axis=-1, keepdims=True)
        pv = _dot(e.astype(BF16), windows[t][1](lo)) * (1.0 / denom)
        o_ref[t * ATT_TQ:(t + 1) * ATT_TQ, lo:lo + LANES] = jnp.where(
            first_head, pv[0:ATT_TQ], pv[ATT_TQ:2 * ATT_TQ]).astype(BF16)


def _attn_kernel(q_ref, k_ref, v_ref, bias_ref, o_ref):
    step = pl.program_id(1)

    @pl.when(step == 0)
    def _():
        windows = [((lambda lo, n=(t + 1) * ATT_TQ: k_ref[0:n, lo:lo + LANES]),
                    (lambda lo, n=(t + 1) * ATT_TQ: v_ref[0:n, lo:lo + LANES]),
                    (t + 1) * ATT_TQ) for t in range(ATT_STEP_TILES)]
        _attention_tiles(q_ref, bias_ref, o_ref, windows)

    @pl.when(step > 0)
    def _():
        windows = []
        for t in range(ATT_STEP_TILES):
            start = pl.multiple_of(step * (ATT_STEP_TILES * ATT_TQ) + t * ATT_TQ - ATT_BACK, ATT_TQ)
            windows.append(((lambda lo, s=start: k_ref[pl.ds(s, ATT_WIN), lo:lo + LANES]),
                            (lambda lo, s=start: v_ref[pl.ds(s, ATT_WIN), lo:lo + LANES]),
                            ATT_WIN))
        _attention_tiles(q_ref, bias_ref, o_ref, windows)


def _block_diag2(a, b):
    z = jnp.zeros_like(a)
    return jnp.concatenate([jnp.concatenate([a, z], axis=1), jnp.concatenate([z, b], axis=1)], axis=0)


def _unit_lower_inverses(l_packed):
    n = l_packed[0].shape[0]
    row = lax.broadcasted_iota(jnp.int32, (n, 2 * n), 0)
    col = lax.broadcasted_iota(jnp.int32, (n, 2 * n), 1) % n
    eye = (row == col).astype(F32)
    base = (row // 2 == col // 2)
    xs = [(eye - jnp.where(base, l2, 0.0)).astype(BF16) for l2 in l_packed]
    lbs = [l2.astype(BF16) for l2 in l_packed]
    b = 2
    while b < n:
        cmask = ((row // (2 * b) == col // (2 * b)) & (row % (2 * b) >= b) & (col % (2 * b) < b)).astype(BF16)
        halves = b % (2 * SUBLANES) == 0
        segs = [(s, s + b) for s in range(b, n, 2 * b)] if halves else [(0, n)]
        def take(a, segs=segs):
            return a if not halves else jnp.concatenate([a[s:e] for s, e in segs], axis=0)

        def put(full, part, segs=segs, halves=halves):
            if not halves:
                return part
            out, at, r = [], 0, 0
            for s, e in segs:
                out += [full[at:s], part[r:r + e - s]]
                at, r = e, r + e - s
            return jnp.concatenate(out, axis=0)

        ys = [_dot(take(lb * cmask), _block_diag2(x[:, 0:n], x[:, n:2 * n])).astype(BF16)
              for lb, x in zip(lbs, xs)]
        ys = [put(jnp.zeros((n, 2 * n), BF16), y) for y in ys]
        zs = [_dot(take(x), _block_diag2(y[:, 0:n], y[:, n:2 * n])) for x, y in zip(xs, ys)]
        xs = [put(x, take(x) - z.astype(BF16)) for x, z in zip(xs, zs)]
        b *= 2
    return xs


def _gdn_kernel(qkv_ref, z_ref, ab_ref, arow_ref, dtrow_ref, gnorm_ref, tri_ref, ea_ref, eb_ref, ones_ref,
                y_ref, state_ref):
    n = GDN_BLOCK

    @pl.when(pl.program_id(1) == 0)
    def _():
        state_ref[...] = jnp.zeros_like(state_ref)

    row2 = lax.broadcasted_iota(jnp.int32, (n, 2 * n), 0)
    col2 = lax.broadcasted_iota(jnp.int32, (n, 2 * n), 1) % n
    lower2 = row2 >= col2
    strict2 = row2 > col2
    first_head = lax.broadcasted_iota(jnp.int32, (1, LANES), 1) < HEAD_DIM
    head_block = (lax.broadcasted_iota(jnp.int32, (LANES, LANES), 0) // HEAD_DIM
                  == lax.broadcasted_iota(jnp.int32, (LANES, LANES), 1) // HEAD_DIM)
    ones_bd = ones_ref[...]

    blocks = []
    for b in range(GDN_STEP_BLOCKS):
        r0 = b * n
        ab = ab_ref[r0:r0 + n, :]
        g_n = -arow_ref[...] * _softplus(ab + dtrow_ref[...])
        beta_n = 1.0 / (1.0 + jnp.exp(-ab))
        gc_n = _select_rows(tri_ref[...], g_n)
        gc_x = _select_cols(gc_n, ea_ref[...])
        g_last = gc_x[n - 1:n, :]
        blocks.append(dict(
            r0=r0, gc_t=gc_n.T, gc_x=gc_x, beta_x=_select_cols(beta_n, eb_ref[...], n_pieces=2),
            exp_gc=jnp.exp(gc_x), exp_rest=jnp.exp(g_last - gc_x), exp_last=jnp.exp(g_last)))

    chains = []
    for blk in blocks:
        for p in range(N_PAIRS):
            lo = p * LANES
            r0 = blk["r0"]
            q = qkv_ref[r0:r0 + n, lo:lo + LANES].astype(F32)
            k = qkv_ref[r0:r0 + n, D_GROUP + lo:D_GROUP + lo + LANES].astype(F32)
            v = qkv_ref[r0:r0 + n, 2 * D_GROUP + lo:2 * D_GROUP + lo + LANES].astype(F32)
            chains.append(dict(blk=blk, p=p, lo=lo, q=q, k=k, v=v))

    for c in chains:
        c["sq"] = _dot(jnp.concatenate([c["q"] * c["q"], c["k"] * c["k"]], axis=0).astype(BF16), ones_bd)
    for c in chains:
        blk, lo = c["blk"], c["lo"]
        q = c["q"] * lax.rsqrt(c["sq"][0:n] + L2_EPS) * (HEAD_DIM ** -0.5)
        k = c["k"] * lax.rsqrt(c["sq"][n:2 * n] + L2_EPS)
        beta = blk["beta_x"][:, lo:lo + LANES]
        kb = k * beta
        c["vb"] = (c["v"] * beta).astype(BF16)
        c["kw"] = (kb * blk["exp_gc"][:, lo:lo + LANES]).astype(BF16)
        c["q_dec"] = (q * blk["exp_gc"][:, lo:lo + LANES]).astype(BF16)
        c["k_rest"] = (k * blk["exp_rest"][:, lo:lo + LANES]).astype(BF16)
        c["kb_q"] = jnp.concatenate([kb.astype(BF16), q.astype(BF16)], axis=0)
        c["kf"] = k.astype(BF16)
    for c in chains:
        c["tt"] = _dot_nt(c["kb_q"], jnp.concatenate(_by_head(first_head, c["kf"]), axis=0))
    l_packed = []
    for c in chains:
        blk, p, lo = c["blk"], c["p"], c["lo"]
        g2 = blk["gc_x"][:, lo:lo + LANES]
        g2_swapped = pltpu.roll(g2, HEAD_DIM, axis=1)
        gcol = jnp.concatenate([jnp.where(first_head, g2, g2_swapped),
                                jnp.where(first_head, g2_swapped, g2)], axis=1)
        grow = jnp.concatenate([jnp.broadcast_to(blk["gc_t"][2 * p:2 * p + 1, :], (n, n)),
                                jnp.broadcast_to(blk["gc_t"][2 * p + 1:2 * p + 2, :], (n, n))], axis=1)
        dec = jnp.exp(jnp.minimum(gcol - grow, 0.0))
        l_packed.append(jnp.where(strict2, c["tt"][0:n] * dec, 0.0))
        c["aqk"] = jnp.where(lower2, c["tt"][n:2 * n] * dec, 0.0).astype(BF16)
    for c, t2 in zip(chains, _unit_lower_inverses(l_packed)):
        vb0, vb1 = _by_head(first_head, c["vb"])
        kw0, kw1 = _by_head(first_head, c["kw"])
        rhs = jnp.concatenate([jnp.concatenate([vb0, kw0], axis=1), jnp.concatenate([vb1, kw1], axis=1)], axis=0)
        c["uw"] = _dot(t2, rhs)

    for b in range(GDN_STEP_BLOCKS):
        cs = chains[b * N_PAIRS:(b + 1) * N_PAIRS]
        for c in cs:
            c["state"] = state_ref[c["p"]]
            w = c["uw"][:, LANES:2 * LANES].astype(BF16)
            c["ws"] = _dot(jnp.concatenate([w, c["q_dec"]], axis=0), c["state"].astype(BF16))
        for c in cs:
            c["v_new"] = (c["uw"][:, 0:LANES] - c["ws"][0:n]).astype(BF16)
        for c in cs:
            c["upd"] = _dot_tn(c["k_rest"], c["v_new"])
            c["intra"] = _dot(c["aqk"], jnp.concatenate(_by_head(first_head, c["v_new"]), axis=0))
        for c in cs:
            lo = c["lo"]
            state_ref[c["p"]] = (c["state"] * c["blk"]["exp_last"][:, lo:lo + LANES]
                                 + jnp.where(head_block, c["upd"], 0.0))
            c["o"] = c["ws"][n:2 * n] + c["intra"]

    for c in chains:
        c["ms"] = _dot((c["o"] * c["o"]).astype(BF16), ones_bd) * (1.0 / HEAD_DIM)
    for c in chains:
        lo, r0 = c["lo"], c["blk"]["r0"]
        z = z_ref[r0:r0 + n, lo:lo + LANES].astype(F32)
        y = c["o"] * lax.rsqrt(c["ms"] + RMS_EPS) * gnorm_ref[:, lo:lo + LANES] * _silu(z)
        y_ref[r0:r0 + n, lo:lo + LANES] = y.astype(BF16)


def _resident(shape):
    zeros = (0,) * len(shape)
    return pl.BlockSpec(shape, lambda *_: zeros, pipeline_mode=pl.Buffered(1))


def _attention_bias(rel_bias):
    span = ATT_WIN + ATT_TQ - 1
    rel = (ATT_WIN - 1) - jnp.arange(span)
    u = rel_bias.astype(F32)[:, jnp.clip(rel, -REL_CLIP, REL_CLIP) + REL_CLIP]
    u = jnp.pad(u, ((0, 0), (0, 1)))
    skew = jnp.tile(u, (1, ATT_TQ))[:, :ATT_TQ * span].reshape(N_HEADS, ATT_TQ, span)
    bias = skew[:, :, ATT_TQ - 1:ATT_TQ - 1 + ATT_WIN]
    bias = (bias - rel_bias.astype(F32)[:, 2 * REL_CLIP][:, None, None]) * LOG2E
    qpos = jnp.arange(ATT_TQ)[:, None]
    kpos = jnp.arange(ATT_WIN)[None, :] - ATT_BACK
    dchunk = qpos // ATT_CHUNK - kpos // ATT_CHUNK
    valid = (dchunk >= 0) & (dchunk <= BAND_CHUNKS)
    return jnp.where(valid[None], bias, NEG_INF).reshape(N_PAIRS, 2 * ATT_TQ, ATT_WIN)


def _layer(x, ffn1_norm, ffn1_w_in, ffn1_w_out, mix_norm, w_in_mix, conv_w, A_log, dt_bias,
           gdn_norm, rel_bias, w_out_mix, ffn2_norm, ffn2_w_in, ffn2_w_out, final_norm):
    bsz, seq, _ = x.shape
    n_tok = bsz * seq
    gdn_rows = GDN_BLOCK * GDN_STEP_BLOCKS
    att_rows = ATT_TQ * ATT_STEP_TILES
    assert seq % ROW_TILE == 0 and seq % gdn_rows == 0 and seq % att_rows == 0 and n_tok % OUT_ROW_TILE == 0
    n_row_tiles = n_tok // ROW_TILE
    row = lambda i: (i, 0)
    params = functools.partial(pltpu.CompilerParams, vmem_limit_bytes=VMEM_LIMIT)

    gq = 3 * D_GROUP
    w_main = jnp.concatenate([w_in_mix[:, 0:4 * D_GROUP], w_in_mix[:, 4 * D_GROUP + 2 * N_HEADS:]], axis=1)
    w_ab = jnp.pad(w_in_mix[:, 4 * D_GROUP:4 * D_GROUP + 2 * N_HEADS], ((0, 0), (0, LANES - 2 * N_HEADS)))

    cur = lambda i: (jnp.minimum(i, n_row_tiles - 1), 0)
    prev = lambda i: (jnp.maximum(i - 1, 0), 0)
    x1, gdn_qkv, gdn_z, ab, att_in = pl.pallas_call(
        functools.partial(_ffn1_proj_kernel, seq // ROW_TILE),
        grid=(n_row_tiles + 1,),
        in_specs=[pl.BlockSpec((ROW_TILE, D_MODEL), cur),
                  _resident((1, D_MODEL)), _resident((D_MODEL, 2 * D_FF)), _resident((D_FF, D_MODEL)),
                  _resident((1, D_MODEL)), _resident((D_MODEL, 7 * D_GROUP)), _resident((D_MODEL, LANES)),
                  _resident((CONV_WIDTH, gq))],
        out_specs=[pl.BlockSpec((ROW_TILE, D_MODEL), cur), pl.BlockSpec((ROW_TILE, gq), prev),
                   pl.BlockSpec((ROW_TILE, D_GROUP), cur), pl.BlockSpec((ROW_TILE, LANES), cur),
                   pl.BlockSpec((ROW_TILE, gq), cur)],
        out_shape=[jax.ShapeDtypeStruct((n_tok, D_MODEL), F32), jax.ShapeDtypeStruct((n_tok, gq), BF16),
                   jax.ShapeDtypeStruct((n_tok, D_GROUP), BF16), jax.ShapeDtypeStruct((n_tok, LANES), F32),
                   jax.ShapeDtypeStruct((n_tok, gq), BF16)],
        scratch_shapes=[pltpu.VMEM((ROW_TILE, D_FF), BF16), pltpu.VMEM((ROW_TILE + CONV_HALO, gq), F32)],
        compiler_params=params(dimension_semantics=("arbitrary",)),
        name="ffn1_proj",
    )(x.reshape(n_tok, D_MODEL), ffn1_norm.reshape(1, D_MODEL).astype(F32),
      ffn1_w_in.astype(BF16), ffn1_w_out.astype(BF16),
      mix_norm.reshape(1, D_MODEL).astype(F32), w_main.astype(BF16), w_ab.astype(BF16), conv_w.astype(F32))

    n = GDN_BLOCK
    head_of_lane = jnp.arange(D_GROUP) // HEAD_DIM
    lane = jnp.arange(LANES)
    tri = (jnp.arange(n)[:, None] >= jnp.arange(n)[None, :]).astype(BF16)
    e_a = (lane[:, None] == head_of_lane[None, :]).astype(BF16)
    e_b = (lane[:, None] == (head_of_lane + N_HEADS)[None, :]).astype(BF16)
    ones_bd = (lane[:, None] // HEAD_DIM == lane[None, :] // HEAD_DIM).astype(BF16)
    a_row = jnp.pad(jnp.exp(A_log.astype(F32)), (0, LANES - N_HEADS)).reshape(1, LANES)
    dt_row = jnp.pad(dt_bias.astype(F32), (0, LANES - N_HEADS)).reshape(1, LANES)
    gnorm_row = jnp.tile(gdn_norm.astype(F32), N_HEADS).reshape(1, D_GROUP)
    blk3 = lambda b, i: (b, i, 0)

    y_gdn = pl.pallas_call(
        _gdn_kernel,
        grid=(bsz, seq // gdn_rows),
        in_specs=[pl.BlockSpec((None, gdn_rows, gq), blk3), pl.BlockSpec((None, gdn_rows, D_GROUP), blk3),
                  pl.BlockSpec((None, gdn_rows, LANES), blk3),
                  _resident((1, LANES)), _resident((1, LANES)), _resident((1, D_GROUP)), _resident((n, n)),
                  _resident((LANES, D_GROUP)), _resident((LANES, D_GROUP)), _resident((LANES, LANES))],
        out_specs=pl.BlockSpec((None, gdn_rows, D_GROUP), blk3),
        out_shape=jax.ShapeDtypeStruct((bsz, seq, D_GROUP), BF16),
        scratch_shapes=[pltpu.VMEM((N_PAIRS, LANES, LANES), F32)],
        compiler_params=params(dimension_semantics=("arbitrary", "arbitrary")),
        name="gated_deltanet",
    )(gdn_qkv.reshape(bsz, seq, gq), gdn_z.reshape(bsz, seq, D_GROUP), ab.reshape(bsz, seq, LANES),
      a_row, dt_row, gnorm_row, tri, e_a, e_b, ones_bd)

    att3 = att_in.reshape(bsz, seq, gq)
    y_att = pl.pallas_call(
        _attn_kernel,
        grid=(bsz, seq // att_rows),
        in_specs=[pl.BlockSpec((None, att_rows, D_GROUP), lambda b, i: (b, i, 0)),
                  pl.BlockSpec((None, seq, D_GROUP), lambda b, i: (b, 0, 1)),
                  pl.BlockSpec((None, seq, D_GROUP), lambda b, i: (b, 0, 2)),
                  _resident((N_PAIRS, 2 * ATT_TQ, ATT_WIN))],
        out_specs=pl.BlockSpec((None, att_rows, D_GROUP), blk3),
        out_shape=jax.ShapeDtypeStruct((bsz, seq, D_GROUP), BF16),
        compiler_params=params(dimension_semantics=("arbitrary", "arbitrary")),
        name="chunk_attention",
    )(att3, att3, att3, _attention_bias(rel_bias))

    out = pl.pallas_call(
        _out_ffn2_kernel,
        grid=(n_tok // OUT_ROW_TILE,),
        in_specs=[pl.BlockSpec((OUT_ROW_TILE, D_MODEL), row), pl.BlockSpec((OUT_ROW_TILE, D_GROUP), row),
                  pl.BlockSpec((OUT_ROW_TILE, D_GROUP), row), _resident((2 * D_GROUP, D_MODEL)),
                  _resident((1, D_MODEL)), _resident((D_MODEL, 2 * D_FF)), _resident((D_FF, D_MODEL)),
                  _resident((1, D_MODEL))],
        out_specs=pl.BlockSpec((OUT_ROW_TILE, D_MODEL), row),
        out_shape=jax.ShapeDtypeStruct((n_tok, D_MODEL), F32),
        scratch_shapes=[pltpu.VMEM((OUT_ROW_TILE, D_FF), BF16)],
        compiler_params=params(dimension_semantics=("arbitrary",)),
        name="out_ffn2_norm",
    )(x1, y_gdn.reshape(n_tok, D_GROUP), y_att.reshape(n_tok, D_GROUP), w_out_mix.astype(BF16),
      ffn2_norm.reshape(1, D_MODEL).astype(F32), ffn2_w_in.astype(BF16), ffn2_w_out.astype(BF16),
      final_norm.reshape(1, D_MODEL).astype(F32))
    return out.reshape(bsz, seq, D_MODEL)


def kernel(x, ffn1_norm, ffn1_w_in, ffn1_w_out, mix_norm, w_in_mix, conv_w, A_log, dt_bias, gdn_norm,
           rel_bias, w_out_mix, ffn2_norm, ffn2_w_in, ffn2_w_out, final_norm):
    assert ffn1_norm.shape[0] == 1, "the final norm is fused into the single layer's last call"
    return _layer(x, ffn1_norm[0], ffn1_w_in[0], ffn1_w_out[0], mix_norm[0], w_in_mix[0], conv_w[0],
                  A_log[0], dt_bias[0], gdn_norm[0], rel_bias[0], w_out_mix[0], ffn2_norm[0],
                  ffn2_w_in[0], ffn2_w_out[0], final_norm)
```

```python
import functools

import jax
import jax.numpy as jnp
from jax import lax
from jax.experimental import pallas as pl
from jax.experimental.pallas import tpu as pltpu

F32 = jnp.float32
BF16 = jnp.bfloat16

D_MODEL = 1024
HEAD_DIM = 64
N_HEADS = 8
D_GROUP = N_HEADS * HEAD_DIM
D_FF = 2816
CONV_WIDTH = 4
ATT_CHUNK = 64
BAND_CHUNKS = 8
REL_CLIP = 128
RMS_EPS = 1e-6
L2_EPS = 1e-6
NEG_INF = -1e30

LANES = 128
SUBLANES = 8
N_PAIRS = D_GROUP // LANES
MXU_COLS = 256
FF_TILE = MXU_COLS
N_FF_TILES = D_FF // FF_TILE
ROW_TILE = 512
OUT_ROW_TILE = 1024
GDN_BLOCK = 128
GDN_STEP_BLOCKS = 8
GDN_GROUP_BLOCKS = 2
ATT_TQ = 128
ATT_BACK = BAND_CHUNKS * ATT_CHUNK
ATT_WIN = ATT_BACK + ATT_TQ
ATT_STEP_TILES = ATT_BACK // ATT_TQ
LOG2E = 1.4426950408889634


def _bias_free_tiles():
    free = []
    for w in range(ATT_WIN // LANES):
        k_lo, k_hi = w * LANES - ATT_BACK, w * LANES - ATT_BACK + LANES - 1
        clipped = 0 - k_hi >= REL_CLIP
        dchunks = [qc - kc for qc in range(ATT_TQ // ATT_CHUNK) for kc in (k_lo // ATT_CHUNK, k_hi // ATT_CHUNK)]
        if clipped and min(dchunks) >= 0 and max(dchunks) <= BAND_CHUNKS:
            free.append(w)
    return tuple(free)


ATT_BIAS_FREE_TILES = _bias_free_tiles()
CONV_HALO = SUBLANES
VMEM_LIMIT = 56 * 1024 * 1024


def _dot(a, b):
    return jnp.dot(a, b, preferred_element_type=F32)


def _dot_nt(a, b):
    return lax.dot_general(a, b, (((1,), (1,)), ((), ())), preferred_element_type=F32)


def _dot_tn(a, b):
    return lax.dot_general(a, b, (((0,), (0,)), ((), ())), preferred_element_type=F32)


def _bf16_pieces(x, n_pieces):
    pieces = []
    for _ in range(n_pieces - 1):
        hi = x.astype(BF16)
        pieces.append(hi)
        x = x - hi.astype(F32)
    pieces.append(x.astype(BF16))
    return pieces


def _select_cols(x, sel, n_pieces=3):
    return functools.reduce(lambda a, b: a + b, [_dot(p, sel) for p in _bf16_pieces(x, n_pieces)])


def _select_rows(sel, x, n_pieces=3):
    return functools.reduce(lambda a, b: a + b, [_dot(sel, p) for p in _bf16_pieces(x, n_pieces)])


def _rmsnorm(x, gain):
    ms = jnp.mean(x * x, axis=-1, keepdims=True)
    return x * lax.rsqrt(ms + RMS_EPS) * gain


def _silu(x):
    return x * (1.0 / (1.0 + jnp.exp(-x)))


def _softplus(x):
    return jnp.maximum(x, 0.0) + jnp.log(1.0 + jnp.exp(-jnp.abs(x)))


def _by_head(first_head, x):
    z = jnp.zeros_like(x)
    return jnp.where(first_head, x, z), jnp.where(first_head, z, x)


def _ffn_half_step(x, gain_ref, win_ref, wout_ref, act_ref):
    h = _rmsnorm(x, gain_ref[...]).astype(BF16)
    for j in range(N_FF_TILES):
        lo = j * FF_TILE
        gate = _dot(h, win_ref[:, lo:lo + FF_TILE])
        up = _dot(h, win_ref[:, D_FF + lo:D_FF + lo + FF_TILE])
        act_ref[:, lo:lo + FF_TILE] = (_silu(gate) * up).astype(BF16)
    return x + 0.5 * _dot(act_ref[...], wout_ref[...])


def _ffn1_proj_kernel(tiles_per_seq, x_ref, g1_ref, win_ref, wout_ref, gm_ref, wmix_ref, wab_ref, convw_ref,
                      x1_ref, qkv_ref, z_ref, ab_ref, att_ref, act_ref, pre_ref):
    i = pl.program_id(0)
    rows = x_ref.shape[0]
    d3 = 3 * D_GROUP

    @pl.when(i == 0)
    def _():
        pre_ref[...] = jnp.zeros_like(pre_ref)

    @pl.when((i + tiles_per_seq - 1) % tiles_per_seq == 0)
    def _():
        pre_ref[0:CONV_HALO, :] = jnp.zeros((CONV_HALO, d3), F32)

    for t in range(3):
        lo = t * D_GROUP
        acc = None
        for j in range(CONV_WIDTH):
            off = CONV_HALO - (CONV_WIDTH - 1) + j
            term = convw_ref[j:j + 1, lo:lo + D_GROUP] * pre_ref[off:off + rows, lo:lo + D_GROUP]
            acc = term if acc is None else acc + term
        qkv_ref[:, lo:lo + D_GROUP] = _silu(acc).astype(BF16)
    pre_ref[0:CONV_HALO, :] = pre_ref[rows:rows + CONV_HALO, :]

    x1 = _ffn_half_step(x_ref[...], g1_ref, win_ref, wout_ref, act_ref)
    x1_ref[...] = x1
    h = _rmsnorm(x1, gm_ref[...]).astype(BF16)
    for t in range(3):
        lo = t * D_GROUP
        pre_ref[CONV_HALO:CONV_HALO + rows, lo:lo + D_GROUP] = _dot(h, wmix_ref[:, lo:lo + D_GROUP])
    z_ref[...] = _dot(h, wmix_ref[:, d3:d3 + D_GROUP]).astype(BF16)
    ab_ref[...] = _dot(h, wab_ref[...])
    for t in range(3):
        lo = t * D_GROUP
        r = _dot(h, wmix_ref[:, 4 * D_GROUP + lo:4 * D_GROUP + lo + D_GROUP])
        if t == 0:
            r = r * (HEAD_DIM ** -0.5 * LOG2E)
        att_ref[:, lo:lo + D_GROUP] = r.astype(BF16)


def _out_ffn2_kernel(x1_ref, yg_ref, ya_ref, wo_ref, g2_ref, win_ref, wout_ref, gf_ref,
                     o_ref, act_ref):
    x2 = (x1_ref[...] + _dot(yg_ref[...], wo_ref[0:D_GROUP, :])
          + _dot(ya_ref[...], wo_ref[D_GROUP:2 * D_GROUP, :]))
    x3 = _ffn_half_step(x2, g2_ref, win_ref, wout_ref, act_ref)
    o_ref[...] = _rmsnorm(x3, gf_ref[...])


def _attention_tiles(q_ref, bias_ref, o_ref, windows):
    first_head = lax.broadcasted_iota(jnp.int32, (1, LANES), 1) < HEAD_DIM
    units = [(t, p) for t in range(len(windows)) for p in range(N_PAIRS)]

    def scores(t, p):
        load_k, _, n_keys = windows[t]
        lo = p * LANES
        q2 = q_ref[t * ATT_TQ:(t + 1) * ATT_TQ, lo:lo + LANES]
        qs = jnp.concatenate(_by_head(first_head, q2), axis=0)
        s = _dot_nt(qs, load_k(lo))
        cols = []
        for w in range((ATT_WIN - n_keys) // LANES, ATT_WIN // LANES):
            c0 = w * LANES - (ATT_WIN - n_keys)
            sw = s[:, c0:c0 + LANES]
            if w not in ATT_BIAS_FREE_TILES:
                sw = sw + bias_ref[p, :, w * LANES:(w + 1) * LANES]
            cols.append(sw)
        return jnp.concatenate(cols, axis=1)

    s_next = scores(*units[0])
    for idx, (t, p) in enumerate(units):
        s = s_next
        if idx + 1 < len(units):
            s_next = scores(*units[idx + 1])
        lo = p * LANES
        m = jnp.max(s, axis=-1, keepdims=True)
        e = jnp.exp2(s - m)
        denom = jnp.sum(e, axis=-1, keepdims=True)
        pv = _dot(e.astype(BF16), windows[t][1](lo)) * (1.0 / denom)
        o_ref[t * ATT_TQ:(t + 1) * ATT_TQ, lo:lo + LANES] = jnp.where(
            first_head, pv[0:ATT_TQ], pv[ATT_TQ:2 * ATT_TQ]).astype(BF16)


def _attn_kernel(q_ref, k_ref, v_ref, bias_ref, o_ref):
    step = pl.program_id(1)

    @pl.when(step == 0)
    def _():
        windows = [((lambda lo, n=(t + 1) * ATT_TQ: k_ref[0:n, lo:lo + LANES]),
                    (lambda lo, n=(t + 1) * ATT_TQ: v_ref[0:n, lo:lo + LANES]),
                    (t + 1) * ATT_TQ) for t in range(ATT_STEP_TILES)]
        _attention_tiles(q_ref, bias_ref, o_ref, windows)

    @pl.when(step > 0)
    def _():
        windows = []
        for t in range(ATT_STEP_TILES):
            start = pl.multiple_of(step * (ATT_STEP_TILES * ATT_TQ) + t * ATT_TQ - ATT_BACK, ATT_TQ)
            windows.append(((lambda lo, s=start: k_ref[pl.ds(s, ATT_WIN), lo:lo + LANES]),
                            (lambda lo, s=start: v_ref[pl.ds(s, ATT_WIN), lo:lo + LANES]),
                            ATT_WIN))
        _attention_tiles(q_ref, bias_ref, o_ref, windows)


def _block_diag2(a, b):
    z = jnp.zeros_like(a)
    return jnp.concatenate([jnp.concatenate([a, z], axis=1), jnp.concatenate([z, b], axis=1)], axis=0)


def _inverse_stages(chains):
    n = GDN_BLOCK
    row = lax.broadcasted_iota(jnp.int32, (n, 2 * n), 0)
    col = lax.broadcasted_iota(jnp.int32, (n, 2 * n), 1) % n

    def init():
        eye = (row == col).astype(F32)
        base = (row // 2 == col // 2)
        for c in chains:
            c["t2"] = (eye - jnp.where(base, c["l2"], 0.0)).astype(BF16)
            c["lb"] = c["l2"].astype(BF16)

    def level(b):
        halves = b % (2 * SUBLANES) == 0
        segs = [(s, s + b) for s in range(b, n, 2 * b)]

        def take(a):
            return a if not halves else jnp.concatenate([a[s:e] for s, e in segs], axis=0)

        def put(full, part):
            if not halves:
                return part
            out, at, r = [], 0, 0
            for s, e in segs:
                out += [full[at:s], part[r:r + e - s]]
                at, r = e, r + e - s
            return jnp.concatenate(out, axis=0)

        def cx():
            cmask = ((row // (2 * b) == col // (2 * b)) & (row % (2 * b) >= b) & (col % (2 * b) < b)).astype(BF16)
            for c in chains:
                x = c["t2"]
                y = _dot(take(c["lb"] * cmask), _block_diag2(x[:, 0:n], x[:, n:2 * n])).astype(BF16)
                c["y"] = put(jnp.zeros((n, 2 * n), BF16), y)

        def xcx():
            for c in chains:
                x, y = c["t2"], c.pop("y")
                z = _dot(take(x), _block_diag2(y[:, 0:n], y[:, n:2 * n]))
                c["t2"] = put(x, take(x) - z.astype(BF16))

        return [cx, xcx]

    stages, b = [init], 2
    while b < n:
        stages += level(b)
        b *= 2
    return stages


def _gdn_stages(block_ids, qkv_ref, z_ref, ab_ref, arow_ref, dtrow_ref, gnorm_ref, tri_ref, ea_ref, eb_ref,
                ones_ref, y_ref, state_ref):
    n = GDN_BLOCK
    row2 = lax.broadcasted_iota(jnp.int32, (n, 2 * n), 0)
    col2 = lax.broadcasted_iota(jnp.int32, (n, 2 * n), 1) % n
    first_head = lax.broadcasted_iota(jnp.int32, (1, LANES), 1) < HEAD_DIM
    chains = []

    def gating():
        for b in block_ids:
            r0 = b * n
            ab = ab_ref[r0:r0 + n, :]
            g_n = -arow_ref[...] * _softplus(ab + dtrow_ref[...])
            beta_n = 1.0 / (1.0 + jnp.exp(-ab))
            gc_n = _select_rows(tri_ref[...], g_n)
            gc_x = _select_cols(gc_n, ea_ref[...])
            g_last = gc_x[n - 1:n, :]
            blk = dict(r0=r0, gc_t=gc_n.T, gc_x=gc_x, beta_x=_select_cols(beta_n, eb_ref[...], n_pieces=2),
                       exp_gc=jnp.exp(gc_x), exp_rest=jnp.exp(g_last - gc_x), exp_last=jnp.exp(g_last))
            for p in range(N_PAIRS):
                chains.append(dict(blk=blk, p=p, lo=p * LANES))

    def squares():
        ones_bd = ones_ref[...]
        for c in chains:
            lo, r0 = c["lo"], c["blk"]["r0"]
            c["q"] = qkv_ref[r0:r0 + n, lo:lo + LANES].astype(F32)
            c["k"] = qkv_ref[r0:r0 + n, D_GROUP + lo:D_GROUP + lo + LANES].astype(F32)
            c["sq"] = _dot(jnp.concatenate([c["q"] * c["q"], c["k"] * c["k"]], axis=0).astype(BF16), ones_bd)

    def token_products():
        for c in chains:
            blk, lo, r0 = c["blk"], c["lo"], c["blk"]["r0"]
            sq = c.pop("sq")
            q = c.pop("q") * lax.rsqrt(sq[0:n] + L2_EPS) * (HEAD_DIM ** -0.5)
            k = c.pop("k") * lax.rsqrt(sq[n:2 * n] + L2_EPS)
            v = qkv_ref[r0:r0 + n, 2 * D_GROUP + lo:2 * D_GROUP + lo + LANES].astype(F32)
            beta = blk["beta_x"][:, lo:lo + LANES]
            kb = k * beta
            c["vb"] = (v * beta).astype(BF16)
            c["kw"] = (kb * blk["exp_gc"][:, lo:lo + LANES]).astype(BF16)
            c["q_dec"] = (q * blk["exp_gc"][:, lo:lo + LANES]).astype(BF16)
            c["k_rest"] = (k * blk["exp_rest"][:, lo:lo + LANES]).astype(BF16)
            kf = k.astype(BF16)
            c["tt"] = _dot_nt(jnp.concatenate([kb.astype(BF16), q.astype(BF16)], axis=0),
                              jnp.concatenate(_by_head(first_head, kf), axis=0))

    def decays():
        lower2 = row2 >= col2
        strict2 = row2 > col2
        for c in chains:
            blk, p, lo = c["blk"], c["p"], c["lo"]
            g2 = blk["gc_x"][:, lo:lo + LANES]
            g2_swapped = pltpu.roll(g2, HEAD_DIM, axis=1)
            gcol = jnp.concatenate([jnp.where(first_head, g2, g2_swapped),
                                    jnp.where(first_head, g2_swapped, g2)], axis=1)
            grow = jnp.concatenate([jnp.broadcast_to(blk["gc_t"][2 * p:2 * p + 1, :], (n, n)),
                                    jnp.broadcast_to(blk["gc_t"][2 * p + 1:2 * p + 2, :], (n, n))], axis=1)
            dec = jnp.exp(jnp.minimum(gcol - grow, 0.0))
            tt = c.pop("tt")
            c["l2"] = jnp.where(strict2, tt[0:n] * dec, 0.0)
            c["aqk"] = jnp.where(lower2, tt[n:2 * n] * dec, 0.0).astype(BF16)

    def solve():
        for c in chains:
            vb0, vb1 = _by_head(first_head, c.pop("vb"))
            kw0, kw1 = _by_head(first_head, c.pop("kw"))
            rhs = jnp.concatenate([jnp.concatenate([vb0, kw0], axis=1),
                                   jnp.concatenate([vb1, kw1], axis=1)], axis=0)
            c["uw"] = _dot(c.pop("t2"), rhs)
            c.pop("l2"), c.pop("lb")

    def state_in(i):
        def run():
            for c in chains[i * N_PAIRS:(i + 1) * N_PAIRS]:
                c["state"] = state_ref[c["p"]]
                w = c["uw"][:, LANES:2 * LANES].astype(BF16)
                c["ws"] = _dot(jnp.concatenate([w, c.pop("q_dec")], axis=0), c["state"].astype(BF16))
        return run

    def state_out(i):
        def run():
            head_block = (lax.broadcasted_iota(jnp.int32, (LANES, LANES), 0) // HEAD_DIM
                          == lax.broadcasted_iota(jnp.int32, (LANES, LANES), 1) // HEAD_DIM)
            for c in chains[i * N_PAIRS:(i + 1) * N_PAIRS]:
                lo = c["lo"]
                ws = c.pop("ws")
                v_new = (c.pop("uw")[:, 0:LANES] - ws[0:n]).astype(BF16)
                upd = _dot_tn(c.pop("k_rest"), v_new)
                intra = _dot(c.pop("aqk"), jnp.concatenate(_by_head(first_head, v_new), axis=0))
                state_ref[c["p"]] = (c.pop("state") * c["blk"]["exp_last"][:, lo:lo + LANES]
                                     + jnp.where(head_block, upd, 0.0))
                c["o"] = ws[n:2 * n] + intra
        return run

    def mean_squares():
        ones_bd = ones_ref[...]
        for c in chains:
            c["ms"] = _dot((c["o"] * c["o"]).astype(BF16), ones_bd) * (1.0 / HEAD_DIM)

    def outputs():
        for c in chains:
            lo, r0 = c["lo"], c["blk"]["r0"]
            z = z_ref[r0:r0 + n, lo:lo + LANES].astype(F32)
            y = c.pop("o") * lax.rsqrt(c.pop("ms") + RMS_EPS) * gnorm_ref[:, lo:lo + LANES] * _silu(z)
            y_ref[r0:r0 + n, lo:lo + LANES] = y.astype(BF16)

    prep = [gating, squares, token_products, decays]
    tail = [solve]
    for i in range(len(block_ids)):
        tail += [state_in(i), state_out(i)]
    return prep, _inverse_stages(chains), tail + [mean_squares, outputs]


def _interleave(a, b):
    out, ia, ib = [], 0, 0
    while ia < len(a) or ib < len(b):
        if ib >= len(b) or (ia < len(a) and ia * len(b) <= ib * len(a)):
            out.append(a[ia])
            ia += 1
        else:
            out.append(b[ib])
            ib += 1
    return out


def _gdn_kernel(qkv_ref, z_ref, ab_ref, arow_ref, dtrow_ref, gnorm_ref, tri_ref, ea_ref, eb_ref, ones_ref,
                y_ref, state_ref):
    @pl.when(pl.program_id(1) == 0)
    def _():
        state_ref[...] = jnp.zeros_like(state_ref)

    refs = (qkv_ref, z_ref, ab_ref, arow_ref, dtrow_ref, gnorm_ref, tri_ref, ea_ref, eb_ref, ones_ref,
            y_ref, state_ref)
    n_groups = GDN_STEP_BLOCKS // GDN_GROUP_BLOCKS
    groups = [_gdn_stages(tuple(range(g * GDN_GROUP_BLOCKS, (g + 1) * GDN_GROUP_BLOCKS)), *refs)
              for g in range(n_groups)]
    order = list(groups[0][0])
    for g in range(n_groups):
        beside = list(groups[g - 1][2]) if g > 0 else []
        if g + 1 < n_groups:
            beside = _interleave(beside, groups[g + 1][0]) if beside else list(groups[g + 1][0])
        order += _interleave(groups[g][1], beside)
    order += groups[-1][2]
    for stage in order:
        stage()


def _resident(shape):
    zeros = (0,) * len(shape)
    return pl.BlockSpec(shape, lambda *_: zeros, pipeline_mode=pl.Buffered(1))


def _attention_bias(rel_bias):
    span = ATT_WIN + ATT_TQ - 1
    rel = (ATT_WIN - 1) - jnp.arange(span)
    u = rel_bias.astype(F32)[:, jnp.clip(rel, -REL_CLIP, REL_CLIP) + REL_CLIP]
    u = jnp.pad(u, ((0, 0), (0, 1)))
    skew = jnp.tile(u, (1, ATT_TQ))[:, :ATT_TQ * span].reshape(N_HEADS, ATT_TQ, span)
    bias = skew[:, :, ATT_TQ - 1:ATT_TQ - 1 + ATT_WIN]
    bias = (bias - rel_bias.astype(F32)[:, 2 * REL_CLIP][:, None, None]) * LOG2E
    qpos = jnp.arange(ATT_TQ)[:, None]
    kpos = jnp.arange(ATT_WIN)[None, :] - ATT_BACK
    dchunk = qpos // ATT_CHUNK - kpos // ATT_CHUNK
    valid = (dchunk >= 0) & (dchunk <= BAND_CHUNKS)
    return jnp.where(valid[None], bias, NEG_INF).reshape(N_PAIRS, 2 * ATT_TQ, ATT_WIN)


def _layer(x, ffn1_norm, ffn1_w_in, ffn1_w_out, mix_norm, w_in_mix, conv_w, A_log, dt_bias,
           gdn_norm, rel_bias, w_out_mix, ffn2_norm, ffn2_w_in, ffn2_w_out, final_norm):
    bsz, seq, _ = x.shape
    n_tok = bsz * seq
    gdn_rows = GDN_BLOCK * GDN_STEP_BLOCKS
    att_rows = ATT_TQ * ATT_STEP_TILES
    assert seq % ROW_TILE == 0 and seq % gdn_rows == 0 and seq % att_rows == 0 and n_tok % OUT_ROW_TILE == 0
    n_row_tiles = n_tok // ROW_TILE
    row = lambda i: (i, 0)
    params = functools.partial(pltpu.CompilerParams, vmem_limit_bytes=VMEM_LIMIT)

    gq = 3 * D_GROUP
    w_main = jnp.concatenate([w_in_mix[:, 0:4 * D_GROUP], w_in_mix[:, 4 * D_GROUP + 2 * N_HEADS:]], axis=1)
    w_ab = jnp.pad(w_in_mix[:, 4 * D_GROUP:4 * D_GROUP + 2 * N_HEADS], ((0, 0), (0, LANES - 2 * N_HEADS)))

    cur = lambda i: (jnp.minimum(i, n_row_tiles - 1), 0)
    prev = lambda i: (jnp.maximum(i - 1, 0), 0)
    x1, gdn_qkv, gdn_z, ab, att_in = pl.pallas_call(
        functools.partial(_ffn1_proj_kernel, seq // ROW_TILE),
        grid=(n_row_tiles + 1,),
        in_specs=[pl.BlockSpec((ROW_TILE, D_MODEL), cur),
                  _resident((1, D_MODEL)), _resident((D_MODEL, 2 * D_FF)), _resident((D_FF, D_MODEL)),
                  _resident((1, D_MODEL)), _resident((D_MODEL, 7 * D_GROUP)), _resident((D_MODEL, LANES)),
                  _resident((CONV_WIDTH, gq))],
        out_specs=[pl.BlockSpec((ROW_TILE, D_MODEL), cur), pl.BlockSpec((ROW_TILE, gq), prev),
                   pl.BlockSpec((ROW_TILE, D_GROUP), cur), pl.BlockSpec((ROW_TILE, LANES), cur),
                   pl.BlockSpec((ROW_TILE, gq), cur)],
        out_shape=[jax.ShapeDtypeStruct((n_tok, D_MODEL), F32), jax.ShapeDtypeStruct((n_tok, gq), BF16),
                   jax.ShapeDtypeStruct((n_tok, D_GROUP), BF16), jax.ShapeDtypeStruct((n_tok, LANES), F32),
                   jax.ShapeDtypeStruct((n_tok, gq), BF16)],
        scratch_shapes=[pltpu.VMEM((ROW_TILE, D_FF), BF16), pltpu.VMEM((ROW_TILE + CONV_HALO, gq), F32)],
        compiler_params=params(dimension_semantics=("arbitrary",)),
        name="ffn1_proj",
    )(x.reshape(n_tok, D_MODEL), ffn1_norm.reshape(1, D_MODEL).astype(F32),
      ffn1_w_in.astype(BF16), ffn1_w_out.astype(BF16),
      mix_norm.reshape(1, D_MODEL).astype(F32), w_main.astype(BF16), w_ab.astype(BF16), conv_w.astype(F32))

    n = GDN_BLOCK
    head_of_lane = jnp.arange(D_GROUP) // HEAD_DIM
    lane = jnp.arange(LANES)
    tri = (jnp.arange(n)[:, None] >= jnp.arange(n)[None, :]).astype(BF16)
    e_a = (lane[:, None] == head_of_lane[None, :]).astype(BF16)
    e_b = (lane[:, None] == (head_of_lane + N_HEADS)[None, :]).astype(BF16)
    ones_bd = (lane[:, None] // HEAD_DIM == lane[None, :] // HEAD_DIM).astype(BF16)
    a_row = jnp.pad(jnp.exp(A_log.astype(F32)), (0, LANES - N_HEADS)).reshape(1, LANES)
    dt_row = jnp.pad(dt_bias.astype(F32), (0, LANES - N_HEADS)).reshape(1, LANES)
    gnorm_row = jnp.tile(gdn_norm.astype(F32), N_HEADS).reshape(1, D_GROUP)
    blk3 = lambda b, i: (b, i, 0)

    y_gdn = pl.pallas_call(
        _gdn_kernel,
        grid=(bsz, seq // gdn_rows),
        in_specs=[pl.BlockSpec((None, gdn_rows, gq), blk3), pl.BlockSpec((None, gdn_rows, D_GROUP), blk3),
                  pl.BlockSpec((None, gdn_rows, LANES), blk3),
                  _resident((1, LANES)), _resident((1, LANES)), _resident((1, D_GROUP)), _resident((n, n)),
                  _resident((LANES, D_GROUP)), _resident((LANES, D_GROUP)), _resident((LANES, LANES))],
        out_specs=pl.BlockSpec((None, gdn_rows, D_GROUP), blk3),
        out_shape=jax.ShapeDtypeStruct((bsz, seq, D_GROUP), BF16),
        scratch_shapes=[pltpu.VMEM((N_PAIRS, LANES, LANES), F32)],
        compiler_params=params(dimension_semantics=("arbitrary", "arbitrary")),
        name="gated_deltanet",
    )(gdn_qkv.reshape(bsz, seq, gq), gdn_z.reshape(bsz, seq, D_GROUP), ab.reshape(bsz, seq, LANES),
      a_row, dt_row, gnorm_row, tri, e_a, e_b, ones_bd)

    att3 = att_in.reshape(bsz, seq, gq)
    y_att = pl.pallas_call(
        _attn_kernel,
        grid=(bsz, seq // att_rows),
        in_specs=[pl.BlockSpec((None, att_rows, D_GROUP), lambda b, i: (b, i, 0)),
                  pl.BlockSpec((None, seq, D_GROUP), lambda b, i: (b, 0, 1)),
                  pl.BlockSpec((None, seq, D_GROUP), lambda b, i: (b, 0, 2)),
                  _resident((N_PAIRS, 2 * ATT_TQ, ATT_WIN))],
        out_specs=pl.BlockSpec((None, att_rows, D_GROUP), blk3),
        out_shape=jax.ShapeDtypeStruct((bsz, seq, D_GROUP), BF16),
        compiler_params=params(dimension_semantics=("arbitrary", "arbitrary")),
        name="chunk_attention",
    )(att3, att3, att3, _attention_bias(rel_bias))

    out = pl.pallas_call(
        _out_ffn2_kernel,
        grid=(n_tok // OUT_ROW_TILE,),
        in_specs=[pl.BlockSpec((OUT_ROW_TILE, D_MODEL), row), pl.BlockSpec((OUT_ROW_TILE, D_GROUP), row),
                  pl.BlockSpec((OUT_ROW_TILE, D_GROUP), row), _resident((2 * D_GROUP, D_MODEL)),
                  _resident((1, D_MODEL)), _resident((D_MODEL, 2 * D_FF)), _resident((D_FF, D_MODEL)),
                  _resident((1, D_MODEL))],
        out_specs=pl.BlockSpec((OUT_ROW_TILE, D_MODEL), row),
        out_shape=jax.ShapeDtypeStruct((n_tok, D_MODEL), F32),
        scratch_shapes=[pltpu.VMEM((OUT_ROW_TILE, D_FF), BF16)],
        compiler_params=params(dimension_semantics=("arbitrary",)),
        name="out_ffn2_norm",
    )(x1, y_gdn.reshape(n_tok, D_GROUP), y_att.reshape(n_tok, D_GROUP), w_out_mix.astype(BF16),
      ffn2_norm.reshape(1, D_MODEL).astype(F32), ffn2_w_in.astype(BF16), ffn2_w_out.astype(BF16),
      final_norm.reshape(1, D_MODEL).astype(F32))
    return out.reshape(bsz, seq, D_MODEL)


def kernel(x, ffn1_norm, ffn1_w_in, ffn1_w_out, mix_norm, w_in_mix, conv_w, A_log, dt_bias, gdn_norm,
           rel_bias, w_out_mix, ffn2_norm, ffn2_w_in, ffn2_w_out, final_norm):
    assert ffn1_norm.shape[0] == 1, "the final norm is fused into the single layer's last call"
    return _layer(x, ffn1_norm[0], ffn1_w_in[0], ffn1_w_out[0], mix_norm[0], w_in_mix[0], conv_w[0],
                  A_log[0], dt_bias[0], gdn_norm[0], rel_bias[0], w_out_mix[0], ffn2_norm[0],
                  ffn2_w_in[0], ffn2_w_out[0], final_norm)
```

```python
import functools

import jax
import jax.numpy as jnp
from jax import lax
from jax.experimental import pallas as pl
from jax.experimental.pallas import tpu as pltpu

F32 = jnp.float32
BF16 = jnp.bfloat16

D_MODEL = 1024
HEAD_DIM = 64
N_HEADS = 8
D_GROUP = N_HEADS * HEAD_DIM
D_FF = 2816
CONV_WIDTH = 4
ATT_CHUNK = 64
BAND_CHUNKS = 8
REL_CLIP = 128
RMS_EPS = 1e-6
L2_EPS = 1e-6
NEG_INF = -1e30

LANES = 128
SUBLANES = 8
N_PAIRS = D_GROUP // LANES
MXU_COLS = 256
FF_TILE = MXU_COLS
N_FF_TILES = D_FF // FF_TILE
ROW_TILE = 512
OUT_ROW_TILE = 1024
GDN_BLOCK = 128
GDN_STEP_BLOCKS = 4
GDN_GROUP_BLOCKS = 2
ATT_TQ = 128
ATT_BACK = BAND_CHUNKS * ATT_CHUNK
ATT_WIN = ATT_BACK + ATT_TQ
ATT_STEP_TILES = ATT_BACK // ATT_TQ
LOG2E = 1.4426950408889634


def _bias_free_tiles():
    free = []
    for w in range(ATT_WIN // LANES):
        k_lo, k_hi = w * LANES - ATT_BACK, w * LANES - ATT_BACK + LANES - 1
        clipped = 0 - k_hi >= REL_CLIP
        dchunks = [qc - kc for qc in range(ATT_TQ // ATT_CHUNK) for kc in (k_lo // ATT_CHUNK, k_hi // ATT_CHUNK)]
        if clipped and min(dchunks) >= 0 and max(dchunks) <= BAND_CHUNKS:
            free.append(w)
    return tuple(free)


ATT_BIAS_FREE_TILES = _bias_free_tiles()
CONV_HALO = SUBLANES
VMEM_LIMIT = 56 * 1024 * 1024


def _dot(a, b):
    return jnp.dot(a, b, preferred_element_type=F32)


def _dot_nt(a, b):
    return lax.dot_general(a, b, (((1,), (1,)), ((), ())), preferred_element_type=F32)


def _dot_tn(a, b):
    return lax.dot_general(a, b, (((0,), (0,)), ((), ())), preferred_element_type=F32)


def _bf16_pieces(x, n_pieces):
    pieces = []
    for _ in range(n_pieces - 1):
        hi = x.astype(BF16)
        pieces.append(hi)
        x = x - hi.astype(F32)
    pieces.append(x.astype(BF16))
    return pieces


def _select_cols(x, sel, n_pieces=3):
    return functools.reduce(lambda a, b: a + b, [_dot(p, sel) for p in _bf16_pieces(x, n_pieces)])


def _select_rows(sel, x, n_pieces=3):
    return functools.reduce(lambda a, b: a + b, [_dot(sel, p) for p in _bf16_pieces(x, n_pieces)])


def _rmsnorm(x, gain):
    ms = jnp.mean(x * x, axis=-1, keepdims=True)
    return x * lax.rsqrt(ms + RMS_EPS) * gain


def _silu(x):
    return x * (1.0 / (1.0 + jnp.exp(-x)))


def _softplus(x):
    return jnp.maximum(x, 0.0) + jnp.log(1.0 + jnp.exp(-jnp.abs(x)))


def _by_head(first_head, x):
    z = jnp.zeros_like(x)
    return jnp.where(first_head, x, z), jnp.where(first_head, z, x)


def _ffn_half_step(x, gain_ref, win_ref, wout_ref, act_ref):
    h = _rmsnorm(x, gain_ref[...]).astype(BF16)
    for j in range(N_FF_TILES):
        lo = j * FF_TILE
        gate = _dot(h, win_ref[:, lo:lo + FF_TILE])
        up = _dot(h, win_ref[:, D_FF + lo:D_FF + lo + FF_TILE])
        act_ref[:, lo:lo + FF_TILE] = (_silu(gate) * up).astype(BF16)
    return x + 0.5 * _dot(act_ref[...], wout_ref[...])


def _ffn1_proj_kernel(tiles_per_seq, x_ref, g1_ref, win_ref, wout_ref, gm_ref, wmix_ref, wab_ref, convw_ref,
                      x1_ref, qkv_ref, z_ref, ab_ref, att_ref, act_ref, pre_ref):
    i = pl.program_id(0)
    rows = x_ref.shape[0]
    d3 = 3 * D_GROUP

    @pl.when(i == 0)
    def _():
        pre_ref[...] = jnp.zeros_like(pre_ref)

    @pl.when((i + tiles_per_seq - 1) % tiles_per_seq == 0)
    def _():
        pre_ref[0:CONV_HALO, :] = jnp.zeros((CONV_HALO, d3), F32)

    for t in range(3):
        lo = t * D_GROUP
        xp = pre_ref[:, lo:lo + D_GROUP]
        acc = convw_ref[0:1, lo:lo + D_GROUP] * xp
        for j in range(1, CONV_WIDTH):
            acc = pltpu.roll(acc, 1, axis=0) + convw_ref[j:j + 1, lo:lo + D_GROUP] * xp
        qkv_ref[:, lo:lo + D_GROUP] = _silu(acc[CONV_HALO:CONV_HALO + rows]).astype(BF16)
    pre_ref[0:CONV_HALO, :] = pre_ref[rows:rows + CONV_HALO, :]

    x1 = _ffn_half_step(x_ref[...], g1_ref, win_ref, wout_ref, act_ref)
    x1_ref[...] = x1
    h = _rmsnorm(x1, gm_ref[...]).astype(BF16)
    for t in range(3):
        lo = t * D_GROUP
        pre_ref[CONV_HALO:CONV_HALO + rows, lo:lo + D_GROUP] = _dot(h, wmix_ref[:, lo:lo + D_GROUP])
    z_ref[...] = _dot(h, wmix_ref[:, d3:d3 + D_GROUP]).astype(BF16)
    ab_ref[...] = _dot(h, wab_ref[...])
    for t in range(3):
        lo = t * D_GROUP
        r = _dot(h, wmix_ref[:, 4 * D_GROUP + lo:4 * D_GROUP + lo + D_GROUP])
        if t == 0:
            r = r * (HEAD_DIM ** -0.5 * LOG2E)
        att_ref[:, lo:lo + D_GROUP] = r.astype(BF16)


def _out_ffn2_kernel(x1_ref, yg_ref, ya_ref, wo_ref, g2_ref, win_ref, wout_ref, gf_ref,
                     o_ref, act_ref):
    x2 = (x1_ref[...] + _dot(yg_ref[...], wo_ref[0:D_GROUP, :])
          + _dot(ya_ref[...], wo_ref[D_GROUP:2 * D_GROUP, :]))
    x3 = _ffn_half_step(x2, g2_ref, win_ref, wout_ref, act_ref)
    o_ref[...] = _rmsnorm(x3, gf_ref[...])


def _attention_tiles(q_ref, bias_ref, o_ref, windows):
    first_head = lax.broadcasted_iota(jnp.int32, (1, LANES), 1) < HEAD_DIM
    units = [(t, p) for t in range(len(windows)) for p in range(N_PAIRS)]

    def scores(t, p):
        load_k, _, n_keys = windows[t]
        lo = p * LANES
        q2 = q_ref[t * ATT_TQ:(t + 1) * ATT_TQ, lo:lo + LANES]
        qs = jnp.concatenate(_by_head(first_head, q2), axis=0)
        s = _dot_nt(qs, load_k(lo))
        cols = []
        for w in range((ATT_WIN - n_keys) // LANES, ATT_WIN // LANES):
            c0 = w * LANES - (ATT_WIN - n_keys)
            sw = s[:, c0:c0 + LANES]
            if w not in ATT_BIAS_FREE_TILES:
                sw = sw + bias_ref[p, :, w * LANES:(w + 1) * LANES]
            cols.append(sw)
        return jnp.concatenate(cols, axis=1)

    s_next = scores(*units[0])
    for idx, (t, p) in enumerate(units):
        s = s_next
        if idx + 1 < len(units):
            s_next = scores(*units[idx + 1])
        lo = p * LANES
        m = jnp.max(s, axis=-1, keepdims=True)
        e = jnp.exp2(s - m)
        denom = jnp.sum(e, axis=-1, keepdims=True)
        pv = _dot(e.astype(BF16), windows[t][1](lo)) * (1.0 / denom)
        o_ref[t * ATT_TQ:(t + 1) * ATT_TQ, lo:lo + LANES] = jnp.where(
            first_head, pv[0:ATT_TQ], pv[ATT_TQ:2 * ATT_TQ]).astype(BF16)


def _attn_kernel(q_ref, k_ref, v_ref, bias_ref, o_ref):
    step = pl.program_id(1)

    @pl.when(step == 0)
    def _():
        windows = [((lambda lo, n=(t + 1) * ATT_TQ: k_ref[0:n, lo:lo + LANES]),
                    (lambda lo, n=(t + 1) * ATT_TQ: v_ref[0:n, lo:lo + LANES]),
                    (t + 1) * ATT_TQ) for t in range(ATT_STEP_TILES)]
        _attention_tiles(q_ref, bias_ref, o_ref, windows)

    @pl.when(step > 0)
    def _():
        windows = []
        for t in range(ATT_STEP_TILES):
            start = pl.multiple_of(step * (ATT_STEP_TILES * ATT_TQ) + t * ATT_TQ - ATT_BACK, ATT_TQ)
            windows.append(((lambda lo, s=start: k_ref[pl.ds(s, ATT_WIN), lo:lo + LANES]),
                            (lambda lo, s=start: v_ref[pl.ds(s, ATT_WIN), lo:lo + LANES]),
                            ATT_WIN))
        _attention_tiles(q_ref, bias_ref, o_ref, windows)


def _block_diag2(a, b):
    z = jnp.zeros_like(a)
    return jnp.concatenate([jnp.concatenate([a, z], axis=1), jnp.concatenate([z, b], axis=1)], axis=0)


def _inverse_stages(chains):
    n = GDN_BLOCK
    row = lax.broadcasted_iota(jnp.int32, (n, 2 * n), 0)
    col = lax.broadcasted_iota(jnp.int32, (n, 2 * n), 1) % n

    def init():
        eye = (row == col).astype(F32)
        base = (row // 2 == col // 2)
        for c in chains:
            c["t2"] = (eye - jnp.where(base, c["l2"], 0.0)).astype(BF16)
            c["lb"] = c["l2"].astype(BF16)

    def level(b):
        halves = b % (2 * SUBLANES) == 0
        segs = [(s, s + b) for s in range(b, n, 2 * b)]

        def take(a):
            return a if not halves else jnp.concatenate([a[s:e] for s, e in segs], axis=0)

        def put(full, part):
            if not halves:
                return part
            out, at, r = [], 0, 0
            for s, e in segs:
                out += [full[at:s], part[r:r + e - s]]
                at, r = e, r + e - s
            return jnp.concatenate(out, axis=0)

        def cx():
            cmask = ((row // (2 * b) == col // (2 * b)) & (row % (2 * b) >= b) & (col % (2 * b) < b)).astype(BF16)
            for c in chains:
                x = c["t2"]
                y = _dot(take(c["lb"] * cmask), _block_diag2(x[:, 0:n], x[:, n:2 * n])).astype(BF16)
                c["y"] = put(jnp.zeros((n, 2 * n), BF16), y)

        def xcx():
            for c in chains:
                x, y = c["t2"], c.pop("y")
                z = _dot(take(x), _block_diag2(y[:, 0:n], y[:, n:2 * n]))
                c["t2"] = put(x, take(x) - z.astype(BF16))

        return [cx, xcx]

    stages, b = [init], 2
    while b < n:
        stages += level(b)
        b *= 2
    return stages


def _gdn_stages(block_ids, qkv_ref, z_ref, ab_ref, arow_ref, dtrow_ref, gnorm_ref, tri_ref, ea_ref, eb_ref,
                ones_ref, y_ref, state_ref):
    n = GDN_BLOCK
    row2 = lax.broadcasted_iota(jnp.int32, (n, 2 * n), 0)
    col2 = lax.broadcasted_iota(jnp.int32, (n, 2 * n), 1) % n
    first_head = lax.broadcasted_iota(jnp.int32, (1, LANES), 1) < HEAD_DIM
    chains = []

    def gating():
        for b in block_ids:
            r0 = b * n
            ab = ab_ref[r0:r0 + n, :]
            g_n = -arow_ref[...] * _softplus(ab + dtrow_ref[...])
            beta_n = 1.0 / (1.0 + jnp.exp(-ab))
            gc_n = _select_rows(tri_ref[...], g_n)
            gc_x = _select_cols(gc_n, ea_ref[...])
            g_last = gc_x[n - 1:n, :]
            blk = dict(r0=r0, gc_t=gc_n.T, gc_x=gc_x, beta_x=_select_cols(beta_n, eb_ref[...], n_pieces=2),
                       exp_gc=jnp.exp(gc_x), exp_rest=jnp.exp(g_last - gc_x), exp_last=jnp.exp(g_last))
            for p in range(N_PAIRS):
                chains.append(dict(blk=blk, p=p, lo=p * LANES))

    def squares():
        ones_bd = ones_ref[...]
        for c in chains:
            lo, r0 = c["lo"], c["blk"]["r0"]
            c["q"] = qkv_ref[r0:r0 + n, lo:lo + LANES].astype(F32)
            c["k"] = qkv_ref[r0:r0 + n, D_GROUP + lo:D_GROUP + lo + LANES].astype(F32)
            c["sq"] = _dot(jnp.concatenate([c["q"] * c["q"], c["k"] * c["k"]], axis=0).astype(BF16), ones_bd)

    def token_products():
        for c in chains:
            blk, lo, r0 = c["blk"], c["lo"], c["blk"]["r0"]
            sq = c.pop("sq")
            q = c.pop("q") * lax.rsqrt(sq[0:n] + L2_EPS) * (HEAD_DIM ** -0.5)
            k = c.pop("k") * lax.rsqrt(sq[n:2 * n] + L2_EPS)
            v = qkv_ref[r0:r0 + n, 2 * D_GROUP + lo:2 * D_GROUP + lo + LANES].astype(F32)
            beta = blk["beta_x"][:, lo:lo + LANES]
            kb = k * beta
            c["vb"] = (v * beta).astype(BF16)
            c["kw"] = (kb * blk["exp_gc"][:, lo:lo + LANES]).astype(BF16)
            c["q_dec"] = (q * blk["exp_gc"][:, lo:lo + LANES]).astype(BF16)
            c["k_rest"] = (k * blk["exp_rest"][:, lo:lo + LANES]).astype(BF16)
            kf = k.astype(BF16)
            c["tt"] = _dot_nt(jnp.concatenate([kb.astype(BF16), q.astype(BF16)], axis=0),
                              jnp.concatenate(_by_head(first_head, kf), axis=0))

    def decays():
        lower2 = row2 >= col2
        strict2 = row2 > col2
        for c in chains:
            blk, p, lo = c["blk"], c["p"], c["lo"]
            g2 = blk["gc_x"][:, lo:lo + LANES]
            g2_swapped = pltpu.roll(g2, HEAD_DIM, axis=1)
            gcol = jnp.concatenate([jnp.where(first_head, g2, g2_swapped),
                                    jnp.where(first_head, g2_swapped, g2)], axis=1)
            grow = jnp.concatenate([jnp.broadcast_to(blk["gc_t"][2 * p:2 * p + 1, :], (n, n)),
                                    jnp.broadcast_to(blk["gc_t"][2 * p + 1:2 * p + 2, :], (n, n))], axis=1)
            dec = jnp.exp(jnp.minimum(gcol - grow, 0.0))
            tt = c.pop("tt")
            c["l2"] = jnp.where(strict2, tt[0:n] * dec, 0.0)
            c["aqk"] = jnp.where(lower2, tt[n:2 * n] * dec, 0.0).astype(BF16)

    def solve():
        for c in chains:
            vb0, vb1 = _by_head(first_head, c.pop("vb"))
            kw0, kw1 = _by_head(first_head, c.pop("kw"))
            rhs = jnp.concatenate([jnp.concatenate([vb0, kw0], axis=1),
                                   jnp.concatenate([vb1, kw1], axis=1)], axis=0)
            c["uw"] = _dot(c.pop("t2"), rhs)
            c.pop("l2"), c.pop("lb")

    def state_in(i):
        def run():
            for c in chains[i * N_PAIRS:(i + 1) * N_PAIRS]:
                c["state"] = state_ref[c["p"]]
                w = c["uw"][:, LANES:2 * LANES].astype(BF16)
                c["ws"] = _dot(jnp.concatenate([w, c.pop("q_dec")], axis=0), c["state"].astype(BF16))
        return run

    def state_out(i):
        def run():
            head_block = (lax.broadcasted_iota(jnp.int32, (LANES, LANES), 0) // HEAD_DIM
                          == lax.broadcasted_iota(jnp.int32, (LANES, LANES), 1) // HEAD_DIM)
            for c in chains[i * N_PAIRS:(i + 1) * N_PAIRS]:
                lo = c["lo"]
                ws = c.pop("ws")
                v_new = (c.pop("uw")[:, 0:LANES] - ws[0:n]).astype(BF16)
                upd = _dot_tn(c.pop("k_rest"), v_new)
                intra = _dot(c.pop("aqk"), jnp.concatenate(_by_head(first_head, v_new), axis=0))
                state_ref[c["p"]] = (c.pop("state") * c["blk"]["exp_last"][:, lo:lo + LANES]
                                     + jnp.where(head_block, upd, 0.0))
                c["o"] = ws[n:2 * n] + intra
        return run

    def mean_squares():
        ones_bd = ones_ref[...]
        for c in chains:
            c["ms"] = _dot((c["o"] * c["o"]).astype(BF16), ones_bd) * (1.0 / HEAD_DIM)

    def outputs():
        for c in chains:
            lo, r0 = c["lo"], c["blk"]["r0"]
            z = z_ref[r0:r0 + n, lo:lo + LANES].astype(F32)
            y = c.pop("o") * lax.rsqrt(c.pop("ms") + RMS_EPS) * gnorm_ref[:, lo:lo + LANES] * _silu(z)
            y_ref[r0:r0 + n, lo:lo + LANES] = y.astype(BF16)

    prep = [gating, squares, token_products, decays]
    tail = [solve]
    for i in range(len(block_ids)):
        tail += [state_in(i), state_out(i)]
    return prep, _inverse_stages(chains), tail + [mean_squares, outputs]


def _interleave(a, b):
    out, ia, ib = [], 0, 0
    while ia < len(a) or ib < len(b):
        if ib >= len(b) or (ia < len(a) and ia * len(b) <= ib * len(a)):
            out.append(a[ia])
            ia += 1
        else:
            out.append(b[ib])
            ib += 1
    return out


def _gdn_kernel(qkv_ref, z_ref, ab_ref, arow_ref, dtrow_ref, gnorm_ref, tri_ref, ea_ref, eb_ref, ones_ref,
                y_ref, state_ref):
    @pl.when(pl.program_id(1) == 0)
    def _():
        state_ref[...] = jnp.zeros_like(state_ref)

    refs = (qkv_ref, z_ref, ab_ref, arow_ref, dtrow_ref, gnorm_ref, tri_ref, ea_ref, eb_ref, ones_ref,
            y_ref, state_ref)
    n_groups = GDN_STEP_BLOCKS // GDN_GROUP_BLOCKS
    groups = [_gdn_stages(tuple(range(g * GDN_GROUP_BLOCKS, (g + 1) * GDN_GROUP_BLOCKS)), *refs)
              for g in range(n_groups)]
    order = list(groups[0][0])
    for g in range(n_groups):
        beside = list(groups[g - 1][2]) if g > 0 else []
        if g + 1 < n_groups:
            beside = _interleave(beside, groups[g + 1][0]) if beside else list(groups[g + 1][0])
        order += _interleave(groups[g][1], beside)
    order += groups[-1][2]
    for stage in order:
        stage()


def _resident(shape):
    zeros = (0,) * len(shape)
    return pl.BlockSpec(shape, lambda *_: zeros, pipeline_mode=pl.Buffered(1))


def _attention_bias(rel_bias):
    span = ATT_WIN + ATT_TQ - 1
    rel = (ATT_WIN - 1) - jnp.arange(span)
    u = rel_bias.astype(F32)[:, jnp.clip(rel, -REL_CLIP, REL_CLIP) + REL_CLIP]
    u = jnp.pad(u, ((0, 0), (0, 1)))
    skew = jnp.tile(u, (1, ATT_TQ))[:, :ATT_TQ * span].reshape(N_HEADS, ATT_TQ, span)
    bias = skew[:, :, ATT_TQ - 1:ATT_TQ - 1 + ATT_WIN]
    bias = (bias - rel_bias.astype(F32)[:, 2 * REL_CLIP][:, None, None]) * LOG2E
    qpos = jnp.arange(ATT_TQ)[:, None]
    kpos = jnp.arange(ATT_WIN)[None, :] - ATT_BACK
    dchunk = qpos // ATT_CHUNK - kpos // ATT_CHUNK
    valid = (dchunk >= 0) & (dchunk <= BAND_CHUNKS)
    return jnp.where(valid[None], bias, NEG_INF).reshape(N_PAIRS, 2 * ATT_TQ, ATT_WIN)


def _layer(x, ffn1_norm, ffn1_w_in, ffn1_w_out, mix_norm, w_in_mix, conv_w, A_log, dt_bias,
           gdn_norm, rel_bias, w_out_mix, ffn2_norm, ffn2_w_in, ffn2_w_out, final_norm):
    bsz, seq, _ = x.shape
    n_tok = bsz * seq
    gdn_rows = GDN_BLOCK * GDN_STEP_BLOCKS
    att_rows = ATT_TQ * ATT_STEP_TILES
    assert seq % ROW_TILE == 0 and seq % gdn_rows == 0 and seq % att_rows == 0 and n_tok % OUT_ROW_TILE == 0
    n_row_tiles = n_tok // ROW_TILE
    row = lambda i: (i, 0)
    params = functools.partial(pltpu.CompilerParams, vmem_limit_bytes=VMEM_LIMIT)

    gq = 3 * D_GROUP
    w_main = jnp.concatenate([w_in_mix[:, 0:4 * D_GROUP], w_in_mix[:, 4 * D_GROUP + 2 * N_HEADS:]], axis=1)
    w_ab = jnp.pad(w_in_mix[:, 4 * D_GROUP:4 * D_GROUP + 2 * N_HEADS], ((0, 0), (0, LANES - 2 * N_HEADS)))

    cur = lambda i: (jnp.minimum(i, n_row_tiles - 1), 0)
    prev = lambda i: (jnp.maximum(i - 1, 0), 0)
    x1, gdn_qkv, gdn_z, ab, att_in = pl.pallas_call(
        functools.partial(_ffn1_proj_kernel, seq // ROW_TILE),
        grid=(n_row_tiles + 1,),
        in_specs=[pl.BlockSpec((ROW_TILE, D_MODEL), cur),
                  _resident((1, D_MODEL)), _resident((D_MODEL, 2 * D_FF)), _resident((D_FF, D_MODEL)),
                  _resident((1, D_MODEL)), _resident((D_MODEL, 7 * D_GROUP)), _resident((D_MODEL, LANES)),
                  _resident((CONV_WIDTH, gq))],
        out_specs=[pl.BlockSpec((ROW_TILE, D_MODEL), cur), pl.BlockSpec((ROW_TILE, gq), prev),
                   pl.BlockSpec((ROW_TILE, D_GROUP), cur), pl.BlockSpec((ROW_TILE, LANES), cur),
                   pl.BlockSpec((ROW_TILE, gq), cur)],
        out_shape=[jax.ShapeDtypeStruct((n_tok, D_MODEL), F32), jax.ShapeDtypeStruct((n_tok, gq), BF16),
                   jax.ShapeDtypeStruct((n_tok, D_GROUP), BF16), jax.ShapeDtypeStruct((n_tok, LANES), F32),
                   jax.ShapeDtypeStruct((n_tok, gq), BF16)],
        scratch_shapes=[pltpu.VMEM((ROW_TILE, D_FF), BF16), pltpu.VMEM((ROW_TILE + CONV_HALO, gq), F32)],
        compiler_params=params(dimension_semantics=("arbitrary",)),
        name="ffn1_proj",
    )(x.reshape(n_tok, D_MODEL), ffn1_norm.reshape(1, D_MODEL).astype(F32),
      ffn1_w_in.astype(BF16), ffn1_w_out.astype(BF16),
      mix_norm.reshape(1, D_MODEL).astype(F32), w_main.astype(BF16), w_ab.astype(BF16), conv_w.astype(F32))

    n = GDN_BLOCK
    head_of_lane = jnp.arange(D_GROUP) // HEAD_DIM
    lane = jnp.arange(LANES)
    tri = (jnp.arange(n)[:, None] >= jnp.arange(n)[None, :]).astype(BF16)
    e_a = (lane[:, None] == head_of_lane[None, :]).astype(BF16)
    e_b = (lane[:, None] == (head_of_lane + N_HEADS)[None, :]).astype(BF16)
    ones_bd = (lane[:, None] // HEAD_DIM == lane[None, :] // HEAD_DIM).astype(BF16)
    a_row = jnp.pad(jnp.exp(A_log.astype(F32)), (0, LANES - N_HEADS)).reshape(1, LANES)
    dt_row = jnp.pad(dt_bias.astype(F32), (0, LANES - N_HEADS)).reshape(1, LANES)
    gnorm_row = jnp.tile(gdn_norm.astype(F32), N_HEADS).reshape(1, D_GROUP)
    blk3 = lambda b, i: (b, i, 0)

    y_gdn = pl.pallas_call(
        _gdn_kernel,
        grid=(bsz, seq // gdn_rows),
        in_specs=[pl.BlockSpec((None, gdn_rows, gq), blk3), pl.BlockSpec((None, gdn_rows, D_GROUP), blk3),
                  pl.BlockSpec((None, gdn_rows, LANES), blk3),
                  _resident((1, LANES)), _resident((1, LANES)), _resident((1, D_GROUP)), _resident((n, n)),
                  _resident((LANES, D_GROUP)), _resident((LANES, D_GROUP)), _resident((LANES, LANES))],
        out_specs=pl.BlockSpec((None, gdn_rows, D_GROUP), blk3),
        out_shape=jax.ShapeDtypeStruct((bsz, seq, D_GROUP), BF16),
        scratch_shapes=[pltpu.VMEM((N_PAIRS, LANES, LANES), F32)],
        compiler_params=params(dimension_semantics=("arbitrary", "arbitrary")),
        name="gated_deltanet",
    )(gdn_qkv.reshape(bsz, seq, gq), gdn_z.reshape(bsz, seq, D_GROUP), ab.reshape(bsz, seq, LANES),
      a_row, dt_row, gnorm_row, tri, e_a, e_b, ones_bd)

    att3 = att_in.reshape(bsz, seq, gq)
    y_att = pl.pallas_call(
        _attn_kernel,
        grid=(bsz, seq // att_rows),
        in_specs=[pl.BlockSpec((None, att_rows, D_GROUP), lambda b, i: (b, i, 0)),
                  pl.BlockSpec((None, seq, D_GROUP), lambda b, i: (b, 0, 1)),
                  pl.BlockSpec((None, seq, D_GROUP), lambda b, i: (b, 0, 2)),
                  _resident((N_PAIRS, 2 * ATT_TQ, ATT_WIN))],
        out_specs=pl.BlockSpec((None, att_rows, D_GROUP), blk3),
        out_shape=jax.ShapeDtypeStruct((bsz, seq, D_GROUP), BF16),
        compiler_params=params(dimension_semantics=("arbitrary", "arbitrary")),
        name="chunk_attention",
    )(att3, att3, att3, _attention_bias(rel_bias))

    out = pl.pallas_call(
        _out_ffn2_kernel,
        grid=(n_tok // OUT_ROW_TILE,),
        in_specs=[pl.BlockSpec((OUT_ROW_TILE, D_MODEL), row), pl.BlockSpec((OUT_ROW_TILE, D_GROUP), row),
                  pl.BlockSpec((OUT_ROW_TILE, D_GROUP), row), _resident((2 * D_GROUP, D_MODEL)),
                  _resident((1, D_MODEL)), _resident((D_MODEL, 2 * D_FF)), _resident((D_FF, D_MODEL)),
                  _resident((1, D_MODEL))],
        out_specs=pl.BlockSpec((OUT_ROW_TILE, D_MODEL), row),
        out_shape=jax.ShapeDtypeStruct((n_tok, D_MODEL), F32),
        scratch_shapes=[pltpu.VMEM((OUT_ROW_TILE, D_FF), BF16)],
        compiler_params=params(dimension_semantics=("arbitrary",)),
        name="out_ffn2_norm",
    )(x1, y_gdn.reshape(n_tok, D_GROUP), y_att.reshape(n_tok, D_GROUP), w_out_mix.astype(BF16),
      ffn2_norm.reshape(1, D_MODEL).astype(F32), ffn2_w_in.astype(BF16), ffn2_w_out.astype(BF16),
      final_norm.reshape(1, D_MODEL).astype(F32))
    return out.reshape(bsz, seq, D_MODEL)


def kernel(x, ffn1_norm, ffn1_w_in, ffn1_w_out, mix_norm, w_in_mix, conv_w, A_log, dt_bias, gdn_norm,
           rel_bias, w_out_mix, ffn2_norm, ffn2_w_in, ffn2_w_out, final_norm):
    assert ffn1_norm.shape[0] == 1, "the final norm is fused into the single layer's last call"
    return _layer(x, ffn1_norm[0], ffn1_w_in[0], ffn1_w_out[0], mix_norm[0], w_in_mix[0], conv_w[0],
                  A_log[0], dt_bias[0], gdn_norm[0], rel_bias[0], w_out_mix[0], ffn2_norm[0],
                  ffn2_w_in[0], ffn2_w_out[0], final_norm)
```

```python
import functools

import jax
import jax.numpy as jnp
from jax import lax
from jax.experimental import pallas as pl
from jax.experimental.pallas import tpu as pltpu

F32 = jnp.float32
BF16 = jnp.bfloat16

D_MODEL = 1024
HEAD_DIM = 64
N_HEADS = 8
D_GROUP = N_HEADS * HEAD_DIM
D_FF = 2816
CONV_WIDTH = 4
ATT_CHUNK = 64
BAND_CHUNKS = 8
REL_CLIP = 128
RMS_EPS = 1e-6
L2_EPS = 1e-6
NEG_INF = -1e30

LANES = 128
SUBLANES = 8
N_PAIRS = D_GROUP // LANES
MXU_COLS = 256
FF_TILE = MXU_COLS
N_FF_TILES = D_FF // FF_TILE
ROW_TILE = 512
OUT_ROW_TILE = 1024
GDN_BLOCK = 128
GDN_STEP_BLOCKS = 8
GDN_GROUP_BLOCKS = 2
ATT_TQ = 128
ATT_BACK = BAND_CHUNKS * ATT_CHUNK
ATT_WIN = ATT_BACK + ATT_TQ
ATT_STEP_TILES = ATT_BACK // ATT_TQ
LOG2E = 1.4426950408889634


def _bias_free_tiles():
    free = []
    for w in range(ATT_WIN // LANES):
        k_lo, k_hi = w * LANES - ATT_BACK, w * LANES - ATT_BACK + LANES - 1
        clipped = 0 - k_hi >= REL_CLIP
        dchunks = [qc - kc for qc in range(ATT_TQ // ATT_CHUNK) for kc in (k_lo // ATT_CHUNK, k_hi // ATT_CHUNK)]
        if clipped and min(dchunks) >= 0 and max(dchunks) <= BAND_CHUNKS:
            free.append(w)
    return tuple(free)


ATT_BIAS_FREE_TILES = _bias_free_tiles()
CONV_HALO = SUBLANES
VMEM_LIMIT = 56 * 1024 * 1024


def _dot(a, b):
    return jnp.dot(a, b, preferred_element_type=F32)


def _dot_nt(a, b):
    return lax.dot_general(a, b, (((1,), (1,)), ((), ())), preferred_element_type=F32)


def _dot_tn(a, b):
    return lax.dot_general(a, b, (((0,), (0,)), ((), ())), preferred_element_type=F32)


def _bf16_pieces(x, n_pieces):
    pieces = []
    for _ in range(n_pieces - 1):
        hi = x.astype(BF16)
        pieces.append(hi)
        x = x - hi.astype(F32)
    pieces.append(x.astype(BF16))
    return pieces


def _select_cols(x, sel, n_pieces=3):
    return functools.reduce(lambda a, b: a + b, [_dot(p, sel) for p in _bf16_pieces(x, n_pieces)])


def _select_rows(sel, x, n_pieces=3):
    return functools.reduce(lambda a, b: a + b, [_dot(sel, p) for p in _bf16_pieces(x, n_pieces)])


def _rmsnorm(x, gain):
    ms = jnp.mean(x * x, axis=-1, keepdims=True)
    return x * lax.rsqrt(ms + RMS_EPS) * gain


def _silu(x):
    return x * (1.0 / (1.0 + jnp.exp(-x)))


def _softplus(x):
    return jnp.maximum(x, 0.0) + jnp.log(1.0 + jnp.exp(-jnp.abs(x)))


def _by_head(first_head, x):
    z = jnp.zeros_like(x)
    return jnp.where(first_head, x, z), jnp.where(first_head, z, x)


def _ffn_half_step(x, gain_ref, win_ref, wout_ref, act_ref):
    h = _rmsnorm(x, gain_ref[...]).astype(BF16)
    for j in range(N_FF_TILES):
        lo = j * FF_TILE
        gate = _dot(h, win_ref[:, lo:lo + FF_TILE])
        up = _dot(h, win_ref[:, D_FF + lo:D_FF + lo + FF_TILE])
        act_ref[:, lo:lo + FF_TILE] = (_silu(gate) * up).astype(BF16)
    return x + 0.5 * _dot(act_ref[...], wout_ref[...])


def _ffn1_proj_kernel(tiles_per_seq, x_ref, g1_ref, win_ref, wout_ref, gm_ref, wmix_ref, wab_ref, convw_ref,
                      x1_ref, qkv_ref, z_ref, ab_ref, att_ref, act_ref, pre_ref):
    i = pl.program_id(0)
    rows = x_ref.shape[0]
    d3 = 3 * D_GROUP

    @pl.when(i == 0)
    def _():
        pre_ref[...] = jnp.zeros_like(pre_ref)

    @pl.when((i + tiles_per_seq - 1) % tiles_per_seq == 0)
    def _():
        pre_ref[0:CONV_HALO, :] = jnp.zeros((CONV_HALO, d3), F32)

    for t in range(3):
        lo = t * D_GROUP
        xp = pre_ref[:, lo:lo + D_GROUP]
        acc = convw_ref[0:1, lo:lo + D_GROUP] * xp
        for j in range(1, CONV_WIDTH):
            acc = pltpu.roll(acc, 1, axis=0) + convw_ref[j:j + 1, lo:lo + D_GROUP] * xp
        qkv_ref[:, lo:lo + D_GROUP] = _silu(acc[CONV_HALO:CONV_HALO + rows]).astype(BF16)
    pre_ref[0:CONV_HALO, :] = pre_ref[rows:rows + CONV_HALO, :]

    x1 = _ffn_half_step(x_ref[...], g1_ref, win_ref, wout_ref, act_ref)
    x1_ref[...] = x1
    h = _rmsnorm(x1, gm_ref[...]).astype(BF16)
    for t in range(3):
        lo = t * D_GROUP
        pre_ref[CONV_HALO:CONV_HALO + rows, lo:lo + D_GROUP] = _dot(h, wmix_ref[:, lo:lo + D_GROUP])
    z_ref[...] = _dot(h, wmix_ref[:, d3:d3 + D_GROUP]).astype(BF16)
    ab_ref[...] = _dot(h, wab_ref[...])
    for t in range(3):
        lo = t * D_GROUP
        r = _dot(h, wmix_ref[:, 4 * D_GROUP + lo:4 * D_GROUP + lo + D_GROUP])
        if t == 0:
            r = r * (HEAD_DIM ** -0.5 * LOG2E)
        att_ref[:, lo:lo + D_GROUP] = r.astype(BF16)


def _out_ffn2_kernel(x1_ref, yg_ref, ya_ref, wo_ref, g2_ref, win_ref, wout_ref, gf_ref,
                     o_ref, act_ref):
    x2 = (x1_ref[...] + _dot(yg_ref[...], wo_ref[0:D_GROUP, :])
          + _dot(ya_ref[...], wo_ref[D_GROUP:2 * D_GROUP, :]))
    x3 = _ffn_half_step(x2, g2_ref, win_ref, wout_ref, act_ref)
    o_ref[...] = _rmsnorm(x3, gf_ref[...])


def _attention_tiles(q_ref, bias_ref, o_ref, windows):
    first_head = lax.broadcasted_iota(jnp.int32, (1, LANES), 1) < HEAD_DIM
    units = [(t, p) for t in range(len(windows)) for p in range(N_PAIRS)]

    def scores(t, p):
        load_k, _, n_keys = windows[t]
        lo = p * LANES
        q2 = q_ref[t * ATT_TQ:(t + 1) * ATT_TQ, lo:lo + LANES]
        qs = jnp.concatenate(_by_head(first_head, q2), axis=0)
        s = _dot_nt(qs, load_k(lo))
        cols = []
        for w in range((ATT_WIN - n_keys) // LANES, ATT_WIN // LANES):
            c0 = w * LANES - (ATT_WIN - n_keys)
            sw = s[:, c0:c0 + LANES]
            if w not in ATT_BIAS_FREE_TILES:
                sw = sw + bias_ref[p, :, w * LANES:(w + 1) * LANES]
            cols.append(sw)
        return jnp.concatenate(cols, axis=1)

    s_next = scores(*units[0])
    for idx, (t, p) in enumerate(units):
        s = s_next
        if idx + 1 < len(units):
            s_next = scores(*units[idx + 1])
        lo = p * LANES
        m = jnp.max(s, axis=-1, keepdims=True)
        e = jnp.exp2(s - m)
        denom = jnp.sum(e, axis=-1, keepdims=True)
        pv = _dot(e.astype(BF16), windows[t][1](lo)) * (1.0 / denom)
        o_ref[t * ATT_TQ:(t + 1) * ATT_TQ, lo:lo + LANES] = jnp.where(
            first_head, pv[0:ATT_TQ], pv[ATT_TQ:2 * ATT_TQ]).astype(BF16)


def _attn_kernel(q_ref, k_ref, v_ref, bias_ref, o_ref):
    step = pl.program_id(1)

    @pl.when(step == 0)
    def _():
        windows = [((lambda lo, n=(t + 1) * ATT_TQ: k_ref[0:n, lo:lo + LANES]),
                    (lambda lo, n=(t + 1) * ATT_TQ: v_ref[0:n, lo:lo + LANES]),
                    (t + 1) * ATT_TQ) for t in range(ATT_STEP_TILES)]
        _attention_tiles(q_ref, bias_ref, o_ref, windows)

    @pl.when(step > 0)
    def _():
        windows = []
        for t in range(ATT_STEP_TILES):
            start = pl.multiple_of(step * (ATT_STEP_TILES * ATT_TQ) + t * ATT_TQ - ATT_BACK, ATT_TQ)
            windows.append(((lambda lo, s=start: k_ref[pl.ds(s, ATT_WIN), lo:lo + LANES]),
                            (lambda lo, s=start: v_ref[pl.ds(s, ATT_WIN), lo:lo + LANES]),
                            ATT_WIN))
        _attention_tiles(q_ref, bias_ref, o_ref, windows)


def _block_diag2(a, b):
    z = jnp.zeros_like(a)
    return jnp.concatenate([jnp.concatenate([a, z], axis=1), jnp.concatenate([z, b], axis=1)], axis=0)


def _inverse_stages(chains):
    n = GDN_BLOCK
    row = lax.broadcasted_iota(jnp.int32, (n, 2 * n), 0)
    col = lax.broadcasted_iota(jnp.int32, (n, 2 * n), 1) % n

    def init():
        eye = (row == col).astype(F32)
        base = (row // 2 == col // 2)
        for c in chains:
            c["t2"] = (eye - jnp.where(base, c["l2"], 0.0)).astype(BF16)
            c["lb"] = c["l2"].astype(BF16)

    def level(b):
        halves = b % (2 * SUBLANES) == 0
        segs = [(s, s + b) for s in range(b, n, 2 * b)]

        def take(a):
            return a if not halves else jnp.concatenate([a[s:e] for s, e in segs], axis=0)

        def put(full, part):
            if not halves:
                return part
            out, at, r = [], 0, 0
            for s, e in segs:
                out += [full[at:s], part[r:r + e - s]]
                at, r = e, r + e - s
            return jnp.concatenate(out, axis=0)

        def cx():
            cmask = ((row // (2 * b) == col // (2 * b)) & (row % (2 * b) >= b) & (col % (2 * b) < b)).astype(BF16)
            for c in chains:
                x = c["t2"]
                y = _dot(take(c["lb"] * cmask), _block_diag2(x[:, 0:n], x[:, n:2 * n])).astype(BF16)
                c["y"] = put(jnp.zeros((n, 2 * n), BF16), y)

        def xcx():
            for c in chains:
                x, y = c["t2"], c.pop("y")
                z = _dot(take(x), _block_diag2(y[:, 0:n], y[:, n:2 * n]))
                c["t2"] = put(x, take(x) - z.astype(BF16))

        return [cx, xcx]

    stages, b = [init], 2
    while b < n:
        stages += level(b)
        b *= 2
    return stages


def _gdn_stages(block_ids, qkv_ref, z_ref, ab_ref, arow_ref, dtrow_ref, gnorm_ref, tri_ref, ea_ref, eb_ref,
                ones_ref, y_ref, state_ref):
    n = GDN_BLOCK
    row2 = lax.broadcasted_iota(jnp.int32, (n, 2 * n), 0)
    col2 = lax.broadcasted_iota(jnp.int32, (n, 2 * n), 1) % n
    first_head = lax.broadcasted_iota(jnp.int32, (1, LANES), 1) < HEAD_DIM
    chains = []

    def gating():
        for b in block_ids:
            r0 = b * n
            ab = ab_ref[r0:r0 + n, :]
            g_n = -arow_ref[...] * _softplus(ab + dtrow_ref[...])
            beta_n = 1.0 / (1.0 + jnp.exp(-ab))
            gc_n = _select_rows(tri_ref[...], g_n)
            gc_x = _select_cols(gc_n, ea_ref[...])
            g_last = gc_x[n - 1:n, :]
            blk = dict(r0=r0, gc_t=gc_n.T, gc_x=gc_x, beta_x=_select_cols(beta_n, eb_ref[...], n_pieces=2),
                       exp_gc=jnp.exp(gc_x), exp_rest=jnp.exp(g_last - gc_x), exp_last=jnp.exp(g_last))
            for p in range(N_PAIRS):
                chains.append(dict(blk=blk, p=p, lo=p * LANES))

    def squares():
        ones_bd = ones_ref[...]
        for c in chains:
            lo, r0 = c["lo"], c["blk"]["r0"]
            c["q"] = qkv_ref[r0:r0 + n, lo:lo + LANES].astype(F32)
            c["k"] = qkv_ref[r0:r0 + n, D_GROUP + lo:D_GROUP + lo + LANES].astype(F32)
            c["sq"] = _dot(jnp.concatenate([c["q"] * c["q"], c["k"] * c["k"]], axis=0).astype(BF16), ones_bd)

    def token_products():
        for c in chains:
            blk, lo, r0 = c["blk"], c["lo"], c["blk"]["r0"]
            sq = c.pop("sq")
            q = c.pop("q") * lax.rsqrt(sq[0:n] + L2_EPS) * (HEAD_DIM ** -0.5)
            k = c.pop("k") * lax.rsqrt(sq[n:2 * n] + L2_EPS)
            v = qkv_ref[r0:r0 + n, 2 * D_GROUP + lo:2 * D_GROUP + lo + LANES].astype(F32)
            beta = blk["beta_x"][:, lo:lo + LANES]
            kb = k * beta
            c["vb"] = (v * beta).astype(BF16)
            c["kw"] = (kb * blk["exp_gc"][:, lo:lo + LANES]).astype(BF16)
            c["q_dec"] = (q * blk["exp_gc"][:, lo:lo + LANES]).astype(BF16)
            c["k_rest"] = (k * blk["exp_rest"][:, lo:lo + LANES]).astype(BF16)
            kf = k.astype(BF16)
            c["tt"] = _dot_nt(jnp.concatenate([kb.astype(BF16), q.astype(BF16)], axis=0),
                              jnp.concatenate(_by_head(first_head, kf), axis=0))

    def decays():
        lower2 = row2 >= col2
        strict2 = row2 > col2
        for c in chains:
            blk, p, lo = c["blk"], c["p"], c["lo"]
            g2 = blk["gc_x"][:, lo:lo + LANES]
            g2_swapped = pltpu.roll(g2, HEAD_DIM, axis=1)
            gcol = jnp.concatenate([jnp.where(first_head, g2, g2_swapped),
                                    jnp.where(first_head, g2_swapped, g2)], axis=1)
            grow = jnp.concatenate([jnp.broadcast_to(blk["gc_t"][2 * p:2 * p + 1, :], (n, n)),
                                    jnp.broadcast_to(blk["gc_t"][2 * p + 1:2 * p + 2, :], (n, n))], axis=1)
            dec = jnp.exp(jnp.minimum(gcol - grow, 0.0))
            tt = c.pop("tt")
            c["l2"] = jnp.where(strict2, tt[0:n] * dec, 0.0)
            c["aqk"] = jnp.where(lower2, tt[n:2 * n] * dec, 0.0).astype(BF16)

    def solve():
        for c in chains:
            vb0, vb1 = _by_head(first_head, c.pop("vb"))
            kw0, kw1 = _by_head(first_head, c.pop("kw"))
            rhs = jnp.concatenate([jnp.concatenate([vb0, kw0], axis=1),
                                   jnp.concatenate([vb1, kw1], axis=1)], axis=0)
            c["uw"] = _dot(c.pop("t2"), rhs)
            c.pop("l2"), c.pop("lb")

    def state_in(i):
        def run():
            for c in chains[i * N_PAIRS:(i + 1) * N_PAIRS]:
                c["state"] = state_ref[c["p"]]
                w = c["uw"][:, LANES:2 * LANES].astype(BF16)
                c["ws"] = _dot(jnp.concatenate([w, c.pop("q_dec")], axis=0), c["state"].astype(BF16))
        return run

    def state_out(i):
        def run():
            head_block = (lax.broadcasted_iota(jnp.int32, (LANES, LANES), 0) // HEAD_DIM
                          == lax.broadcasted_iota(jnp.int32, (LANES, LANES), 1) // HEAD_DIM)
            for c in chains[i * N_PAIRS:(i + 1) * N_PAIRS]:
                lo = c["lo"]
                ws = c.pop("ws")
                v_new = (c.pop("uw")[:, 0:LANES] - ws[0:n]).astype(BF16)
                upd = _dot_tn(c.pop("k_rest"), v_new)
                intra = _dot(c.pop("aqk"), jnp.concatenate(_by_head(first_head, v_new), axis=0))
                state_ref[c["p"]] = (c.pop("state") * c["blk"]["exp_last"][:, lo:lo + LANES]
                                     + jnp.where(head_block, upd, 0.0))
                c["o"] = ws[n:2 * n] + intra
        return run

    def mean_squares():
        ones_bd = ones_ref[...]
        for c in chains:
            c["ms"] = _dot((c["o"] * c["o"]).astype(BF16), ones_bd) * (1.0 / HEAD_DIM)

    def outputs():
        for c in chains:
            lo, r0 = c["lo"], c["blk"]["r0"]
            z = z_ref[r0:r0 + n, lo:lo + LANES].astype(F32)
            y = c.pop("o") * lax.rsqrt(c.pop("ms") + RMS_EPS) * gnorm_ref[:, lo:lo + LANES] * _silu(z)
            y_ref[r0:r0 + n, lo:lo + LANES] = y.astype(BF16)

    prep = [gating, squares, token_products, decays]
    tail = [solve]
    for i in range(len(block_ids)):
        tail += [state_in(i), state_out(i)]
    return prep, _inverse_stages(chains), tail + [mean_squares, outputs]


def _interleave(a, b):
    out, ia, ib = [], 0, 0
    while ia < len(a) or ib < len(b):
        if ib >= len(b) or (ia < len(a) and ia * len(b) <= ib * len(a)):
            out.append(a[ia])
            ia += 1
        else:
            out.append(b[ib])
            ib += 1
    return out


def _gdn_kernel(qkv_ref, z_ref, ab_ref, arow_ref, dtrow_ref, gnorm_ref, tri_ref, ea_ref, eb_ref, ones_ref,
                y_ref, state_ref):
    @pl.when(pl.program_id(1) == 0)
    def _():
        state_ref[...] = jnp.zeros_like(state_ref)

    refs = (qkv_ref, z_ref, ab_ref, arow_ref, dtrow_ref, gnorm_ref, tri_ref, ea_ref, eb_ref, ones_ref,
            y_ref, state_ref)
    n_groups = GDN_STEP_BLOCKS // GDN_GROUP_BLOCKS
    groups = [_gdn_stages(tuple(range(g * GDN_GROUP_BLOCKS, (g + 1) * GDN_GROUP_BLOCKS)), *refs)
              for g in range(n_groups)]
    order = list(groups[0][0])
    for g in range(n_groups):
        beside = list(groups[g - 1][2]) if g > 0 else []
        if g + 1 < n_groups:
            beside = _interleave(beside, groups[g + 1][0]) if beside else list(groups[g + 1][0])
        order += _interleave(groups[g][1], beside)
    order += groups[-1][2]
    for stage in order:
        stage()


def _resident(shape):
    zeros = (0,) * len(shape)
    return pl.BlockSpec(shape, lambda *_: zeros, pipeline_mode=pl.Buffered(1))


def _attention_bias(rel_bias):
    span = ATT_WIN + ATT_TQ - 1
    rel = (ATT_WIN - 1) - jnp.arange(span)
    u = rel_bias.astype(F32)[:, jnp.clip(rel, -REL_CLIP, REL_CLIP) + REL_CLIP]
    u = jnp.pad(u, ((0, 0), (0, 1)))
    skew = jnp.tile(u, (1, ATT_TQ))[:, :ATT_TQ * span].reshape(N_HEADS, ATT_TQ, span)
    bias = skew[:, :, ATT_TQ - 1:ATT_TQ - 1 + ATT_WIN]
    bias = (bias - rel_bias.astype(F32)[:, 2 * REL_CLIP][:, None, None]) * LOG2E
    qpos = jnp.arange(ATT_TQ)[:, None]
    kpos = jnp.arange(ATT_WIN)[None, :] - ATT_BACK
    dchunk = qpos // ATT_CHUNK - kpos // ATT_CHUNK
    valid = (dchunk >= 0) & (dchunk <= BAND_CHUNKS)
    return jnp.where(valid[None], bias, NEG_INF).reshape(N_PAIRS, 2 * ATT_TQ, ATT_WIN)


def _layer(x, ffn1_norm, ffn1_w_in, ffn1_w_out, mix_norm, w_in_mix, conv_w, A_log, dt_bias,
           gdn_norm, rel_bias, w_out_mix, ffn2_norm, ffn2_w_in, ffn2_w_out, final_norm):
    bsz, seq, _ = x.shape
    n_tok = bsz * seq
    gdn_rows = GDN_BLOCK * GDN_STEP_BLOCKS
    att_rows = ATT_TQ * ATT_STEP_TILES
    assert seq % ROW_TILE == 0 and seq % gdn_rows == 0 and seq % att_rows == 0 and n_tok % OUT_ROW_TILE == 0
    n_row_tiles = n_tok // ROW_TILE
    row = lambda i: (i, 0)
    params = functools.partial(pltpu.CompilerParams, vmem_limit_bytes=VMEM_LIMIT)

    gq = 3 * D_GROUP
    w_main = jnp.concatenate([w_in_mix[:, 0:4 * D_GROUP], w_in_mix[:, 4 * D_GROUP + 2 * N_HEADS:]], axis=1)
    w_ab = jnp.pad(w_in_mix[:, 4 * D_GROUP:4 * D_GROUP + 2 * N_HEADS], ((0, 0), (0, LANES - 2 * N_HEADS)))

    cur = lambda i: (jnp.minimum(i, n_row_tiles - 1), 0)
    prev = lambda i: (jnp.maximum(i - 1, 0), 0)
    x1, gdn_qkv, gdn_z, ab, att_in = pl.pallas_call(
        functools.partial(_ffn1_proj_kernel, seq // ROW_TILE),
        grid=(n_row_tiles + 1,),
        in_specs=[pl.BlockSpec((ROW_TILE, D_MODEL), cur),
                  _resident((1, D_MODEL)), _resident((D_MODEL, 2 * D_FF)), _resident((D_FF, D_MODEL)),
                  _resident((1, D_MODEL)), _resident((D_MODEL, 7 * D_GROUP)), _resident((D_MODEL, LANES)),
                  _resident((CONV_WIDTH, gq))],
        out_specs=[pl.BlockSpec((ROW_TILE, D_MODEL), cur), pl.BlockSpec((ROW_TILE, gq), prev),
                   pl.BlockSpec((ROW_TILE, D_GROUP), cur), pl.BlockSpec((ROW_TILE, LANES), cur),
                   pl.BlockSpec((ROW_TILE, gq), cur)],
        out_shape=[jax.ShapeDtypeStruct((n_tok, D_MODEL), F32), jax.ShapeDtypeStruct((n_tok, gq), BF16),
                   jax.ShapeDtypeStruct((n_tok, D_GROUP), BF16), jax.ShapeDtypeStruct((n_tok, LANES), F32),
                   jax.ShapeDtypeStruct((n_tok, gq), BF16)],
        scratch_shapes=[pltpu.VMEM((ROW_TILE, D_FF), BF16), pltpu.VMEM((ROW_TILE + CONV_HALO, gq), F32)],
        compiler_params=params(dimension_semantics=("arbitrary",)),
        name="ffn1_proj",
    )(x.reshape(n_tok, D_MODEL), ffn1_norm.reshape(1, D_MODEL).astype(F32),
      ffn1_w_in.astype(BF16), ffn1_w_out.astype(BF16),
      mix_norm.reshape(1, D_MODEL).astype(F32), w_main.astype(BF16), w_ab.astype(BF16), conv_w.astype(F32))

    n = GDN_BLOCK
    head_of_lane = jnp.arange(D_GROUP) // HEAD_DIM
    lane = jnp.arange(LANES)
    tri = (jnp.arange(n)[:, None] >= jnp.arange(n)[None, :]).astype(BF16)
    e_a = (lane[:, None] == head_of_lane[None, :]).astype(BF16)
    e_b = (lane[:, None] == (head_of_lane + N_HEADS)[None, :]).astype(BF16)
    ones_bd = (lane[:, None] // HEAD_DIM == lane[None, :] // HEAD_DIM).astype(BF16)
    a_row = jnp.pad(jnp.exp(A_log.astype(F32)), (0, LANES - N_HEADS)).reshape(1, LANES)
    dt_row = jnp.pad(dt_bias.astype(F32), (0, LANES - N_HEADS)).reshape(1, LANES)
    gnorm_row = jnp.tile(gdn_norm.astype(F32), N_HEADS).reshape(1, D_GROUP)
    blk3 = lambda b, i: (b, i, 0)

    y_gdn = pl.pallas_call(
        _gdn_kernel,
        grid=(bsz, seq // gdn_rows),
        in_specs=[pl.BlockSpec((None, gdn_rows, gq), blk3), pl.BlockSpec((None, gdn_rows, D_GROUP), blk3),
                  pl.BlockSpec((None, gdn_rows, LANES), blk3),
                  _resident((1, LANES)), _resident((1, LANES)), _resident((1, D_GROUP)), _resident((n, n)),
                  _resident((LANES, D_GROUP)), _resident((LANES, D_GROUP)), _resident((LANES, LANES))],
        out_specs=pl.BlockSpec((None, gdn_rows, D_GROUP), blk3),
        out_shape=jax.ShapeDtypeStruct((bsz, seq, D_GROUP), BF16),
        scratch_shapes=[pltpu.VMEM((N_PAIRS, LANES, LANES), F32)],
        compiler_params=params(dimension_semantics=("arbitrary", "arbitrary")),
        name="gated_deltanet",
    )(gdn_qkv.reshape(bsz, seq, gq), gdn_z.reshape(bsz, seq, D_GROUP), ab.reshape(bsz, seq, LANES),
      a_row, dt_row, gnorm_row, tri, e_a, e_b, ones_bd)

    att3 = att_in.reshape(bsz, seq, gq)
    y_att = pl.pallas_call(
        _attn_kernel,
        grid=(bsz, seq // att_rows),
        in_specs=[pl.BlockSpec((None, att_rows, D_GROUP), lambda b, i: (b, i, 0)),
                  pl.BlockSpec((None, seq, D_GROUP), lambda b, i: (b, 0, 1)),
                  pl.BlockSpec((None, seq, D_GROUP), lambda b, i: (b, 0, 2)),
                  _resident((N_PAIRS, 2 * ATT_TQ, ATT_WIN))],
        out_specs=pl.BlockSpec((None, att_rows, D_GROUP), blk3),
        out_shape=jax.ShapeDtypeStruct((bsz, seq, D_GROUP), BF16),
        compiler_params=params(dimension_semantics=("arbitrary", "arbitrary")),
        name="chunk_attention",
    )(att3, att3, att3, _attention_bias(rel_bias))

    out = pl.pallas_call(
        _out_ffn2_kernel,
        grid=(n_tok // OUT_ROW_TILE,),
        in_specs=[pl.BlockSpec((OUT_ROW_TILE, D_MODEL), row), pl.BlockSpec((OUT_ROW_TILE, D_GROUP), row),
                  pl.BlockSpec((OUT_ROW_TILE, D_GROUP), row), _resident((2 * D_GROUP, D_MODEL)),
                  _resident((1, D_MODEL)), _resident((D_MODEL, 2 * D_FF)), _resident((D_FF, D_MODEL)),
                  _resident((1, D_MODEL))],
        out_specs=pl.BlockSpec((OUT_ROW_TILE, D_MODEL), row),
        out_shape=jax.ShapeDtypeStruct((n_tok, D_MODEL), F32),
        scratch_shapes=[pltpu.VMEM((OUT_ROW_TILE, D_FF), BF16)],
        compiler_params=params(dimension_semantics=("arbitrary",)),
        name="out_ffn2_norm",
    )(x1, y_gdn.reshape(n_tok, D_GROUP), y_att.reshape(n_tok, D_GROUP), w_out_mix.astype(BF16),
      ffn2_norm.reshape(1, D_MODEL).astype(F32), ffn2_w_in.astype(BF16), ffn2_w_out.astype(BF16),
      final_norm.reshape(1, D_MODEL).astype(F32))
    return out.reshape(bsz, seq, D_MODEL)


def kernel(x, ffn1_norm, ffn1_w_in, ffn1_w_out, mix_norm, w_in_mix, conv_w, A_log, dt_bias, gdn_norm,
           rel_bias, w_out_mix, ffn2_norm, ffn2_w_in, ffn2_w_out, final_norm):
    assert ffn1_norm.shape[0] == 1, "the final norm is fused into the single layer's last call"
    return _layer(x, ffn1_norm[0], ffn1_w_in[0], ffn1_w_out[0], mix_norm[0], w_in_mix[0], conv_w[0],
                  A_log[0], dt_bias[0], gdn_norm[0], rel_bias[0], w_out_mix[0], ffn2_norm[0],
                  ffn2_w_in[0], ffn2_w_out[0], final_norm)
```

```python
import functools

import jax
import jax.numpy as jnp
from jax import lax
from jax.experimental import pallas as pl
from jax.experimental.pallas import tpu as pltpu

F32 = jnp.float32
BF16 = jnp.bfloat16

D_MODEL = 1024
HEAD_DIM = 64
N_HEADS = 8
D_GROUP = N_HEADS * HEAD_DIM
D_FF = 2816
CONV_WIDTH = 4
ATT_CHUNK = 64
BAND_CHUNKS = 8
REL_CLIP = 128
RMS_EPS = 1e-6
L2_EPS = 1e-6
NEG_INF = -1e30

LANES = 128
SUBLANES = 8
N_PAIRS = D_GROUP // LANES
MXU_COLS = 256
FF_TILE = MXU_COLS
N_FF_TILES = D_FF // FF_TILE
ROW_TILE = 512
OUT_ROW_TILE = 1024
GDN_BLOCK = 128
GDN_STEP_BLOCKS = 8
GDN_GROUP_BLOCKS = 2
ATT_TQ = 128
ATT_BACK = BAND_CHUNKS * ATT_CHUNK
ATT_WIN = ATT_BACK + ATT_TQ
ATT_STEP_TILES = ATT_BACK // ATT_TQ
LOG2E = 1.4426950408889634


def _bias_free_tiles():
    free = []
    for w in range(ATT_WIN // LANES):
        k_lo, k_hi = w * LANES - ATT_BACK, w * LANES - ATT_BACK + LANES - 1
        clipped = 0 - k_hi >= REL_CLIP
        dchunks = [qc - kc for qc in range(ATT_TQ // ATT_CHUNK) for kc in (k_lo // ATT_CHUNK, k_hi // ATT_CHUNK)]
        if clipped and min(dchunks) >= 0 and max(dchunks) <= BAND_CHUNKS:
            free.append(w)
    return tuple(free)


ATT_BIAS_FREE_TILES = _bias_free_tiles()
CONV_HALO = SUBLANES
VMEM_LIMIT = 56 * 1024 * 1024


def _dot(a, b):
    return jnp.dot(a, b, preferred_element_type=F32)


def _dot_nt(a, b):
    return lax.dot_general(a, b, (((1,), (1,)), ((), ())), preferred_element_type=F32)


def _dot_tn(a, b):
    return lax.dot_general(a, b, (((0,), (0,)), ((), ())), preferred_element_type=F32)


def _bf16_pieces(x, n_pieces):
    pieces = []
    for _ in range(n_pieces - 1):
        hi = x.astype(BF16)
        pieces.append(hi)
        x = x - hi.astype(F32)
    pieces.append(x.astype(BF16))
    return pieces


def _select_cols(x, sel, n_pieces=3):
    return functools.reduce(lambda a, b: a + b, [_dot(p, sel) for p in _bf16_pieces(x, n_pieces)])


def _select_rows(sel, x, n_pieces=3):
    return functools.reduce(lambda a, b: a + b, [_dot(sel, p) for p in _bf16_pieces(x, n_pieces)])


def _rmsnorm(x, gain):
    ms = jnp.mean(x * x, axis=-1, keepdims=True)
    return x * lax.rsqrt(ms + RMS_EPS) * gain


def _silu(x):
    return x * (1.0 / (1.0 + jnp.exp(-x)))


def _softplus(x):
    return jnp.maximum(x, 0.0) + jnp.log(1.0 + jnp.exp(-jnp.abs(x)))


def _by_head(first_head, x):
    z = jnp.zeros_like(x)
    return jnp.where(first_head, x, z), jnp.where(first_head, z, x)


def _ffn_half_step(x, gain_ref, win_ref, wout_ref, act_ref):
    h = _rmsnorm(x, gain_ref[...]).astype(BF16)
    for j in range(N_FF_TILES):
        lo = j * FF_TILE
        gate = _dot(h, win_ref[:, lo:lo + FF_TILE])
        up = _dot(h, win_ref[:, D_FF + lo:D_FF + lo + FF_TILE])
        act_ref[:, lo:lo + FF_TILE] = (_silu(gate) * up).astype(BF16)
    return x + 0.5 * _dot(act_ref[...], wout_ref[...])


def _ffn1_proj_kernel(tiles_per_seq, x_ref, g1_ref, win_ref, wout_ref, gm_ref, wmix_ref, wab_ref, convw_ref,
                      x1_ref, qkv_ref, z_ref, ab_ref, att_ref, act_ref, pre_ref):
    i = pl.program_id(0)
    rows = x_ref.shape[0]
    d3 = 3 * D_GROUP

    @pl.when(i == 0)
    def _():
        pre_ref[...] = jnp.zeros_like(pre_ref)

    @pl.when((i + tiles_per_seq - 1) % tiles_per_seq == 0)
    def _():
        pre_ref[0:CONV_HALO, :] = jnp.zeros((CONV_HALO, d3), F32)

    for t in range(3):
        lo = t * D_GROUP
        xp = pre_ref[:, lo:lo + D_GROUP]
        acc = convw_ref[0:1, lo:lo + D_GROUP] * xp
        for j in range(1, CONV_WIDTH):
            acc = pltpu.roll(acc, 1, axis=0) + convw_ref[j:j + 1, lo:lo + D_GROUP] * xp
        qkv_ref[:, lo:lo + D_GROUP] = _silu(acc[CONV_HALO:CONV_HALO + rows]).astype(BF16)
    pre_ref[0:CONV_HALO, :] = pre_ref[rows:rows + CONV_HALO, :]

    x1 = _ffn_half_step(x_ref[...], g1_ref, win_ref, wout_ref, act_ref)
    x1_ref[...] = x1
    h = _rmsnorm(x1, gm_ref[...]).astype(BF16)
    for t in range(3):
        lo = t * D_GROUP
        pre_ref[CONV_HALO:CONV_HALO + rows, lo:lo + D_GROUP] = _dot(h, wmix_ref[:, lo:lo + D_GROUP])
    z_ref[...] = _dot(h, wmix_ref[:, d3:d3 + D_GROUP]).astype(BF16)
    ab_ref[...] = _dot(h, wab_ref[...])
    for t in range(3):
        lo = t * D_GROUP
        r = _dot(h, wmix_ref[:, 4 * D_GROUP + lo:4 * D_GROUP + lo + D_GROUP])
        if t == 0:
            r = r * (HEAD_DIM ** -0.5 * LOG2E)
        att_ref[:, lo:lo + D_GROUP] = r.astype(BF16)


def _out_ffn2_kernel(x1_ref, yg_ref, ya_ref, wo_ref, g2_ref, win_ref, wout_ref, gf_ref,
                     o_ref, act_ref):
    x2 = (x1_ref[...] + _dot(yg_ref[...], wo_ref[0:D_GROUP, :])
          + _dot(ya_ref[...], wo_ref[D_GROUP:2 * D_GROUP, :]))
    x3 = _ffn_half_step(x2, g2_ref, win_ref, wout_ref, act_ref)
    o_ref[...] = _rmsnorm(x3, gf_ref[...])


def _attention_tiles(q_ref, bias_ref, o_ref, windows):
    first_head = lax.broadcasted_iota(jnp.int32, (1, LANES), 1) < HEAD_DIM
    units = [(t, p) for t in range(len(windows)) for p in range(N_PAIRS)]

    def scores(t, p):
        load_k, _, n_keys = windows[t]
        lo = p * LANES
        q2 = q_ref[t * ATT_TQ:(t + 1) * ATT_TQ, lo:lo + LANES]
        qs = jnp.concatenate(_by_head(first_head, q2), axis=0)
        s = _dot_nt(qs, load_k(lo))
        cols = []
        for w in range((ATT_WIN - n_keys) // LANES, ATT_WIN // LANES):
            c0 = w * LANES - (ATT_WIN - n_keys)
            sw = s[:, c0:c0 + LANES]
            if w not in ATT_BIAS_FREE_TILES:
                sw = sw + bias_ref[p, :, w * LANES:(w + 1) * LANES]
            cols.append(sw)
        return jnp.concatenate(cols, axis=1)

    s_next = scores(*units[0])
    for idx, (t, p) in enumerate(units):
        s = s_next
        if idx + 1 < len(units):
            s_next = scores(*units[idx + 1])
        lo = p * LANES
        m = jnp.max(s, axis=-1, keepdims=True)
        e = jnp.exp2(s - m)
        denom = jnp.sum(e, axis=-1, keepdims=True)
        pv = _dot(e.astype(BF16), windows[t][1](lo)) * (1.0 / denom)
        o_ref[t * ATT_TQ:(t + 1) * ATT_TQ, lo:lo + LANES] = jnp.where(
            first_head, pv[0:ATT_TQ], pv[ATT_TQ:2 * ATT_TQ]).astype(BF16)


def _attn_kernel(q_ref, kp_ref, kc_ref, vp_ref, vc_ref, bias_ref, o_ref):
    step = pl.program_id(1)

    @pl.when(step == 0)
    def _():
        windows = [((lambda lo, n=(t + 1) * ATT_TQ: kc_ref[0:n, lo:lo + LANES]),
                    (lambda lo, n=(t + 1) * ATT_TQ: vc_ref[0:n, lo:lo + LANES]),
                    (t + 1) * ATT_TQ) for t in range(ATT_STEP_TILES)]
        _attention_tiles(q_ref, bias_ref, o_ref, windows)

    @pl.when(step > 0)
    def _():
        def window(prev_ref, cur_ref, t):
            return lambda lo: jnp.concatenate([prev_ref[t * ATT_TQ:ATT_BACK, lo:lo + LANES],
                                               cur_ref[0:(t + 1) * ATT_TQ, lo:lo + LANES]], axis=0)
        windows = [(window(kp_ref, kc_ref, t), window(vp_ref, vc_ref, t), ATT_WIN) for t in range(ATT_STEP_TILES)]
        _attention_tiles(q_ref, bias_ref, o_ref, windows)


def _block_diag2(a, b):
    z = jnp.zeros_like(a)
    return jnp.concatenate([jnp.concatenate([a, z], axis=1), jnp.concatenate([z, b], axis=1)], axis=0)


def _inverse_stages(chains):
    n = GDN_BLOCK
    row = lax.broadcasted_iota(jnp.int32, (n, 2 * n), 0)
    col = lax.broadcasted_iota(jnp.int32, (n, 2 * n), 1) % n

    def init():
        eye = (row == col).astype(F32)
        base = (row // 2 == col // 2)
        for c in chains:
            c["t2"] = (eye - jnp.where(base, c["l2"], 0.0)).astype(BF16)
            c["lb"] = c["l2"].astype(BF16)

    def level(b):
        halves = b % (2 * SUBLANES) == 0
        segs = [(s, s + b) for s in range(b, n, 2 * b)]

        def take(a):
            return a if not halves else jnp.concatenate([a[s:e] for s, e in segs], axis=0)

        def put(full, part):
            if not halves:
                return part
            out, at, r = [], 0, 0
            for s, e in segs:
                out += [full[at:s], part[r:r + e - s]]
                at, r = e, r + e - s
            return jnp.concatenate(out, axis=0)

        def cx():
            cmask = ((row // (2 * b) == col // (2 * b)) & (row % (2 * b) >= b) & (col % (2 * b) < b)).astype(BF16)
            for c in chains:
                x = c["t2"]
                y = _dot(take(c["lb"] * cmask), _block_diag2(x[:, 0:n], x[:, n:2 * n])).astype(BF16)
                c["y"] = put(jnp.zeros((n, 2 * n), BF16), y)

        def xcx():
            for c in chains:
                x, y = c["t2"], c.pop("y")
                z = _dot(take(x), _block_diag2(y[:, 0:n], y[:, n:2 * n]))
                c["t2"] = put(x, take(x) - z.astype(BF16))

        return [cx, xcx]

    stages, b = [init], 2
    while b < n:
        stages += level(b)
        b *= 2
    return stages


def _gdn_stages(block_ids, qkv_ref, z_ref, ab_ref, arow_ref, dtrow_ref, gnorm_ref, tri_ref, ea_ref, eb_ref,
                ones_ref, y_ref, state_ref):
    n = GDN_BLOCK
    row2 = lax.broadcasted_iota(jnp.int32, (n, 2 * n), 0)
    col2 = lax.broadcasted_iota(jnp.int32, (n, 2 * n), 1) % n
    first_head = lax.broadcasted_iota(jnp.int32, (1, LANES), 1) < HEAD_DIM
    chains = []

    def gating():
        for b in block_ids:
            r0 = b * n
            ab = ab_ref[r0:r0 + n, :]
            g_n = -arow_ref[...] * _softplus(ab + dtrow_ref[...])
            beta_n = 1.0 / (1.0 + jnp.exp(-ab))
            gc_n = _select_rows(tri_ref[...], g_n)
            gc_x = _select_cols(gc_n, ea_ref[...])
            g_last = gc_x[n - 1:n, :]
            blk = dict(r0=r0, gc_t=gc_n.T, gc_x=gc_x, beta_x=_select_cols(beta_n, eb_ref[...], n_pieces=2),
                       exp_gc=jnp.exp(gc_x), exp_rest=jnp.exp(g_last - gc_x), exp_last=jnp.exp(g_last))
            for p in range(N_PAIRS):
                chains.append(dict(blk=blk, p=p, lo=p * LANES))

    def squares():
        ones_bd = ones_ref[...]
        for c in chains:
            lo, r0 = c["lo"], c["blk"]["r0"]
            c["q"] = qkv_ref[r0:r0 + n, lo:lo + LANES].astype(F32)
            c["k"] = qkv_ref[r0:r0 + n, D_GROUP + lo:D_GROUP + lo + LANES].astype(F32)
            c["sq"] = _dot(jnp.concatenate([c["q"] * c["q"], c["k"] * c["k"]], axis=0).astype(BF16), ones_bd)

    def token_products():
        for c in chains:
            blk, lo, r0 = c["blk"], c["lo"], c["blk"]["r0"]
            sq = c.pop("sq")
            q = c.pop("q") * lax.rsqrt(sq[0:n] + L2_EPS) * (HEAD_DIM ** -0.5)
            k = c.pop("k") * lax.rsqrt(sq[n:2 * n] + L2_EPS)
            v = qkv_ref[r0:r0 + n, 2 * D_GROUP + lo:2 * D_GROUP + lo + LANES].astype(F32)
            beta = blk["beta_x"][:, lo:lo + LANES]
            kb = k * beta
            c["vb"] = (v * beta).astype(BF16)
            c["kw"] = (kb * blk["exp_gc"][:, lo:lo + LANES]).astype(BF16)
            c["q_dec"] = (q * blk["exp_gc"][:, lo:lo + LANES]).astype(BF16)
            c["k_rest"] = (k * blk["exp_rest"][:, lo:lo + LANES]).astype(BF16)
            kf = k.astype(BF16)
            c["tt"] = _dot_nt(jnp.concatenate([kb.astype(BF16), q.astype(BF16)], axis=0),
                              jnp.concatenate(_by_head(first_head, kf), axis=0))

    def decays():
        lower2 = row2 >= col2
        strict2 = row2 > col2
        for c in chains:
            blk, p, lo = c["blk"], c["p"], c["lo"]
            g2 = blk["gc_x"][:, lo:lo + LANES]
            g2_swapped = pltpu.roll(g2, HEAD_DIM, axis=1)
            gcol = jnp.concatenate([jnp.where(first_head, g2, g2_swapped),
                                    jnp.where(first_head, g2_swapped, g2)], axis=1)
            grow = jnp.concatenate([jnp.broadcast_to(blk["gc_t"][2 * p:2 * p + 1, :], (n, n)),
                                    jnp.broadcast_to(blk["gc_t"][2 * p + 1:2 * p + 2, :], (n, n))], axis=1)
            dec = jnp.exp(jnp.minimum(gcol - grow, 0.0))
            tt = c.pop("tt")
            c["l2"] = jnp.where(strict2, tt[0:n] * dec, 0.0)
            c["aqk"] = jnp.where(lower2, tt[n:2 * n] * dec, 0.0).astype(BF16)

    def solve():
        for c in chains:
            vb0, vb1 = _by_head(first_head, c.pop("vb"))
            kw0, kw1 = _by_head(first_head, c.pop("kw"))
            rhs = jnp.concatenate([jnp.concatenate([vb0, kw0], axis=1),
                                   jnp.concatenate([vb1, kw1], axis=1)], axis=0)
            c["uw"] = _dot(c.pop("t2"), rhs)
            c.pop("l2"), c.pop("lb")

    def state_in(i):
        def run():
            for c in chains[i * N_PAIRS:(i + 1) * N_PAIRS]:
                c["state"] = state_ref[c["p"]]
                w = c["uw"][:, LANES:2 * LANES].astype(BF16)
                c["ws"] = _dot(jnp.concatenate([w, c.pop("q_dec")], axis=0), c["state"].astype(BF16))
        return run

    def state_out(i):
        def run():
            head_block = (lax.broadcasted_iota(jnp.int32, (LANES, LANES), 0) // HEAD_DIM
                          == lax.broadcasted_iota(jnp.int32, (LANES, LANES), 1) // HEAD_DIM)
            for c in chains[i * N_PAIRS:(i + 1) * N_PAIRS]:
                lo = c["lo"]
                ws = c.pop("ws")
                v_new = (c.pop("uw")[:, 0:LANES] - ws[0:n]).astype(BF16)
                upd = _dot_tn(c.pop("k_rest"), v_new)
                intra = _dot(c.pop("aqk"), jnp.concatenate(_by_head(first_head, v_new), axis=0))
                state_ref[c["p"]] = (c.pop("state") * c["blk"]["exp_last"][:, lo:lo + LANES]
                                     + jnp.where(head_block, upd, 0.0))
                c["o"] = ws[n:2 * n] + intra
        return run

    def mean_squares():
        ones_bd = ones_ref[...]
        for c in chains:
            c["ms"] = _dot((c["o"] * c["o"]).astype(BF16), ones_bd) * (1.0 / HEAD_DIM)

    def outputs():
        for c in chains:
            lo, r0 = c["lo"], c["blk"]["r0"]
            z = z_ref[r0:r0 + n, lo:lo + LANES].astype(F32)
            y = c.pop("o") * lax.rsqrt(c.pop("ms") + RMS_EPS) * gnorm_ref[:, lo:lo + LANES] * _silu(z)
            y_ref[r0:r0 + n, lo:lo + LANES] = y.astype(BF16)

    prep = [gating, squares, token_products, decays]
    tail = [solve]
    for i in range(len(block_ids)):
        tail += [state_in(i), state_out(i)]
    return prep, _inverse_stages(chains), tail + [mean_squares, outputs]


def _interleave(a, b):
    out, ia, ib = [], 0, 0
    while ia < len(a) or ib < len(b):
        if ib >= len(b) or (ia < len(a) and ia * len(b) <= ib * len(a)):
            out.append(a[ia])
            ia += 1
        else:
            out.append(b[ib])
            ib += 1
    return out


def _gdn_kernel(qkv_ref, z_ref, ab_ref, arow_ref, dtrow_ref, gnorm_ref, tri_ref, ea_ref, eb_ref, ones_ref,
                y_ref, state_ref):
    @pl.when(pl.program_id(1) == 0)
    def _():
        state_ref[...] = jnp.zeros_like(state_ref)

    refs = (qkv_ref, z_ref, ab_ref, arow_ref, dtrow_ref, gnorm_ref, tri_ref, ea_ref, eb_ref, ones_ref,
            y_ref, state_ref)
    n_groups = GDN_STEP_BLOCKS // GDN_GROUP_BLOCKS
    groups = [_gdn_stages(tuple(range(g * GDN_GROUP_BLOCKS, (g + 1) * GDN_GROUP_BLOCKS)), *refs)
              for g in range(n_groups)]
    order = list(groups[0][0])
    for g in range(n_groups):
        beside = list(groups[g - 1][2]) if g > 0 else []
        if g + 1 < n_groups:
            beside = _interleave(beside, groups[g + 1][0]) if beside else list(groups[g + 1][0])
        order += _interleave(groups[g][1], beside)
    order += groups[-1][2]
    for stage in order:
        stage()


def _resident(shape):
    zeros = (0,) * len(shape)
    return pl.BlockSpec(shape, lambda *_: zeros, pipeline_mode=pl.Buffered(1))


def _attention_bias(rel_bias):
    span = ATT_WIN + ATT_TQ - 1
    rel = (ATT_WIN - 1) - jnp.arange(span)
    u = rel_bias.astype(F32)[:, jnp.clip(rel, -REL_CLIP, REL_CLIP) + REL_CLIP]
    u = jnp.pad(u, ((0, 0), (0, 1)))
    skew = jnp.tile(u, (1, ATT_TQ))[:, :ATT_TQ * span].reshape(N_HEADS, ATT_TQ, span)
    bias = skew[:, :, ATT_TQ - 1:ATT_TQ - 1 + ATT_WIN]
    bias = (bias - rel_bias.astype(F32)[:, 2 * REL_CLIP][:, None, None]) * LOG2E
    qpos = jnp.arange(ATT_TQ)[:, None]
    kpos = jnp.arange(ATT_WIN)[None, :] - ATT_BACK
    dchunk = qpos // ATT_CHUNK - kpos // ATT_CHUNK
    valid = (dchunk >= 0) & (dchunk <= BAND_CHUNKS)
    return jnp.where(valid[None], bias, NEG_INF).reshape(N_PAIRS, 2 * ATT_TQ, ATT_WIN)


def _layer(x, ffn1_norm, ffn1_w_in, ffn1_w_out, mix_norm, w_in_mix, conv_w, A_log, dt_bias,
           gdn_norm, rel_bias, w_out_mix, ffn2_norm, ffn2_w_in, ffn2_w_out, final_norm):
    bsz, seq, _ = x.shape
    n_tok = bsz * seq
    gdn_rows = GDN_BLOCK * GDN_STEP_BLOCKS
    att_rows = ATT_TQ * ATT_STEP_TILES
    assert seq % ROW_TILE == 0 and seq % gdn_rows == 0 and seq % att_rows == 0 and n_tok % OUT_ROW_TILE == 0
    n_row_tiles = n_tok // ROW_TILE
    row = lambda i: (i, 0)
    params = functools.partial(pltpu.CompilerParams, vmem_limit_bytes=VMEM_LIMIT)

    gq = 3 * D_GROUP
    w_mix = w_in_mix.astype(BF16)
    w_main = jnp.concatenate([w_mix[:, 0:4 * D_GROUP], w_mix[:, 4 * D_GROUP + 2 * N_HEADS:]], axis=1)
    w_ab = jnp.pad(w_mix[:, 4 * D_GROUP:4 * D_GROUP + 2 * N_HEADS], ((0, 0), (0, LANES - 2 * N_HEADS)))

    cur = lambda i: (jnp.minimum(i, n_row_tiles - 1), 0)
    prev = lambda i: (jnp.maximum(i - 1, 0), 0)
    x1, gdn_qkv, gdn_z, ab, att_in = pl.pallas_call(
        functools.partial(_ffn1_proj_kernel, seq // ROW_TILE),
        grid=(n_row_tiles + 1,),
        in_specs=[pl.BlockSpec((ROW_TILE, D_MODEL), cur),
                  _resident((1, D_MODEL)), _resident((D_MODEL, 2 * D_FF)), _resident((D_FF, D_MODEL)),
                  _resident((1, D_MODEL)), _resident((D_MODEL, 7 * D_GROUP)), _resident((D_MODEL, LANES)),
                  _resident((CONV_WIDTH, gq))],
        out_specs=[pl.BlockSpec((ROW_TILE, D_MODEL), cur), pl.BlockSpec((ROW_TILE, gq), prev),
                   pl.BlockSpec((ROW_TILE, D_GROUP), cur), pl.BlockSpec((ROW_TILE, LANES), cur),
                   pl.BlockSpec((ROW_TILE, gq), cur)],
        out_shape=[jax.ShapeDtypeStruct((n_tok, D_MODEL), F32), jax.ShapeDtypeStruct((n_tok, gq), BF16),
                   jax.ShapeDtypeStruct((n_tok, D_GROUP), BF16), jax.ShapeDtypeStruct((n_tok, LANES), F32),
                   jax.ShapeDtypeStruct((n_tok, gq), BF16)],
        scratch_shapes=[pltpu.VMEM((ROW_TILE, D_FF), BF16), pltpu.VMEM((ROW_TILE + CONV_HALO, gq), F32)],
        compiler_params=params(dimension_semantics=("arbitrary",)),
        name="ffn1_proj",
    )(x.reshape(n_tok, D_MODEL), ffn1_norm.reshape(1, D_MODEL).astype(F32),
      ffn1_w_in.astype(BF16), ffn1_w_out.astype(BF16),
      mix_norm.reshape(1, D_MODEL).astype(F32), w_main, w_ab, conv_w.astype(F32))

    n = GDN_BLOCK
    head_of_lane = jnp.arange(D_GROUP) // HEAD_DIM
    lane = jnp.arange(LANES)
    tri = (jnp.arange(n)[:, None] >= jnp.arange(n)[None, :]).astype(BF16)
    e_a = (lane[:, None] == head_of_lane[None, :]).astype(BF16)
    e_b = (lane[:, None] == (head_of_lane + N_HEADS)[None, :]).astype(BF16)
    ones_bd = (lane[:, None] // HEAD_DIM == lane[None, :] // HEAD_DIM).astype(BF16)
    a_row = jnp.pad(jnp.exp(A_log.astype(F32)), (0, LANES - N_HEADS)).reshape(1, LANES)
    dt_row = jnp.pad(dt_bias.astype(F32), (0, LANES - N_HEADS)).reshape(1, LANES)
    gnorm_row = jnp.tile(gdn_norm.astype(F32), N_HEADS).reshape(1, D_GROUP)
    blk3 = lambda b, i: (b, i, 0)

    y_gdn = pl.pallas_call(
        _gdn_kernel,
        grid=(bsz, seq // gdn_rows),
        in_specs=[pl.BlockSpec((None, gdn_rows, gq), blk3), pl.BlockSpec((None, gdn_rows, D_GROUP), blk3),
                  pl.BlockSpec((None, gdn_rows, LANES), blk3),
                  _resident((1, LANES)), _resident((1, LANES)), _resident((1, D_GROUP)), _resident((n, n)),
                  _resident((LANES, D_GROUP)), _resident((LANES, D_GROUP)), _resident((LANES, LANES))],
        out_specs=pl.BlockSpec((None, gdn_rows, D_GROUP), blk3),
        out_shape=jax.ShapeDtypeStruct((bsz, seq, D_GROUP), BF16),
        scratch_shapes=[pltpu.VMEM((N_PAIRS, LANES, LANES), F32)],
        compiler_params=params(dimension_semantics=("arbitrary", "arbitrary")),
        name="gated_deltanet",
    )(gdn_qkv.reshape(bsz, seq, gq), gdn_z.reshape(bsz, seq, D_GROUP), ab.reshape(bsz, seq, LANES),
      a_row, dt_row, gnorm_row, tri, e_a, e_b, ones_bd)

    att3 = att_in.reshape(bsz, seq, gq)
    y_att = pl.pallas_call(
        _attn_kernel,
        grid=(bsz, seq // att_rows),
        in_specs=[pl.BlockSpec((None, att_rows, D_GROUP), lambda b, i: (b, i, 0)),
                  pl.BlockSpec((None, att_rows, D_GROUP), lambda b, i: (b, jnp.maximum(i - 1, 0), 1)),
                  pl.BlockSpec((None, att_rows, D_GROUP), lambda b, i: (b, i, 1)),
                  pl.BlockSpec((None, att_rows, D_GROUP), lambda b, i: (b, jnp.maximum(i - 1, 0), 2)),
                  pl.BlockSpec((None, att_rows, D_GROUP), lambda b, i: (b, i, 2)),
                  _resident((N_PAIRS, 2 * ATT_TQ, ATT_WIN))],
        out_specs=pl.BlockSpec((None, att_rows, D_GROUP), blk3),
        out_shape=jax.ShapeDtypeStruct((bsz, seq, D_GROUP), BF16),
        compiler_params=params(dimension_semantics=("arbitrary", "arbitrary")),
        name="chunk_attention",
    )(att3, att3, att3, att3, att3, _attention_bias(rel_bias))

    out = pl.pallas_call(
        _out_ffn2_kernel,
        grid=(n_tok // OUT_ROW_TILE,),
        in_specs=[pl.BlockSpec((OUT_ROW_TILE, D_MODEL), row), pl.BlockSpec((OUT_ROW_TILE, D_GROUP), row),
                  pl.BlockSpec((OUT_ROW_TILE, D_GROUP), row), _resident((2 * D_GROUP, D_MODEL)),
                  _resident((1, D_MODEL)), _resident((D_MODEL, 2 * D_FF)), _resident((D_FF, D_MODEL)),
                  _resident((1, D_MODEL))],
        out_specs=pl.BlockSpec((OUT_ROW_TILE, D_MODEL), row),
        out_shape=jax.ShapeDtypeStruct((n_tok, D_MODEL), F32),
        scratch_shapes=[pltpu.VMEM((OUT_ROW_TILE, D_FF), BF16)],
        compiler_params=params(dimension_semantics=("arbitrary",)),
        name="out_ffn2_norm",
    )(x1, y_gdn.reshape(n_tok, D_GROUP), y_att.reshape(n_tok, D_GROUP), w_out_mix.astype(BF16),
      ffn2_norm.reshape(1, D_MODEL).astype(F32), ffn2_w_in.astype(BF16), ffn2_w_out.astype(BF16),
      final_norm.reshape(1, D_MODEL).astype(F32))
    return out.reshape(bsz, seq, D_MODEL)


def kernel(x, ffn1_norm, ffn1_w_in, ffn1_w_out, mix_norm, w_in_mix, conv_w, A_log, dt_bias, gdn_norm,
           rel_bias, w_out_mix, ffn2_norm, ffn2_w_in, ffn2_w_out, final_norm):
    assert ffn1_norm.shape[0] == 1, "the final norm is fused into the single layer's last call"
    return _layer(x, ffn1_norm[0], ffn1_w_in[0], ffn1_w_out[0], mix_norm[0], w_in_mix[0], conv_w[0],
                  A_log[0], dt_bias[0], gdn_norm[0], rel_bias[0], w_out_mix[0], ffn2_norm[0],
                  ffn2_w_in[0], ffn2_w_out[0], final_norm)
```

```python
import functools

import jax
import jax.numpy as jnp
from jax import lax
from jax.experimental import pallas as pl
from jax.experimental.pallas import tpu as pltpu

F32 = jnp.float32
BF16 = jnp.bfloat16

D_MODEL = 1024
HEAD_DIM = 64
N_HEADS = 8
D_GROUP = N_HEADS * HEAD_DIM
D_FF = 2816
CONV_WIDTH = 4
ATT_CHUNK = 64
BAND_CHUNKS = 8
REL_CLIP = 128
RMS_EPS = 1e-6
L2_EPS = 1e-6
NEG_INF = -1e30

LANES = 128
SUBLANES = 8
N_PAIRS = D_GROUP // LANES
MXU_COLS = 256
FF_TILE = MXU_COLS
N_FF_TILES = D_FF // FF_TILE
ROW_TILE = 512
OUT_ROW_TILE = 1024
GDN_BLOCK = 128
GDN_STEP_BLOCKS = 8
GDN_GROUP_BLOCKS = 8
ATT_TQ = 128
ATT_BACK = BAND_CHUNKS * ATT_CHUNK
ATT_WIN = ATT_BACK + ATT_TQ
ATT_STEP_TILES = ATT_BACK // ATT_TQ
LOG2E = 1.4426950408889634


def _bias_free_tiles():
    free = []
    for w in range(ATT_WIN // LANES):
        k_lo, k_hi = w * LANES - ATT_BACK, w * LANES - ATT_BACK + LANES - 1
        clipped = 0 - k_hi >= REL_CLIP
        dchunks = [qc - kc for qc in range(ATT_TQ // ATT_CHUNK) for kc in (k_lo // ATT_CHUNK, k_hi // ATT_CHUNK)]
        if clipped and min(dchunks) >= 0 and max(dchunks) <= BAND_CHUNKS:
            free.append(w)
    return tuple(free)


ATT_BIAS_FREE_TILES = _bias_free_tiles()
CONV_HALO = SUBLANES
VMEM_LIMIT = 56 * 1024 * 1024


def _dot(a, b):
    return jnp.dot(a, b, preferred_element_type=F32)


def _dot_nt(a, b):
    return lax.dot_general(a, b, (((1,), (1,)), ((), ())), preferred_element_type=F32)


def _dot_tn(a, b):
    return lax.dot_general(a, b, (((0,), (0,)), ((), ())), preferred_element_type=F32)


def _bf16_pieces(x, n_pieces):
    pieces = []
    for _ in range(n_pieces - 1):
        hi = x.astype(BF16)
        pieces.append(hi)
        x = x - hi.astype(F32)
    pieces.append(x.astype(BF16))
    return pieces


def _select_cols(x, sel, n_pieces=3):
    return functools.reduce(lambda a, b: a + b, [_dot(p, sel) for p in _bf16_pieces(x, n_pieces)])


def _select_rows(sel, x, n_pieces=3):
    return functools.reduce(lambda a, b: a + b, [_dot(sel, p) for p in _bf16_pieces(x, n_pieces)])


def _rmsnorm(x, gain):
    ms = jnp.mean(x * x, axis=-1, keepdims=True)
    return x * lax.rsqrt(ms + RMS_EPS) * gain


def _silu(x):
    return x * (1.0 / (1.0 + jnp.exp(-x)))


def _softplus(x):
    return jnp.maximum(x, 0.0) + jnp.log(1.0 + jnp.exp(-jnp.abs(x)))


def _by_head(first_head, x):
    z = jnp.zeros_like(x)
    return jnp.where(first_head, x, z), jnp.where(first_head, z, x)


def _ffn_half_step(x, gain_ref, win_ref, wout_ref, act_ref):
    h = _rmsnorm(x, gain_ref[...]).astype(BF16)
    for j in range(N_FF_TILES):
        lo = j * FF_TILE
        gate = _dot(h, win_ref[:, lo:lo + FF_TILE])
        up = _dot(h, win_ref[:, D_FF + lo:D_FF + lo + FF_TILE])
        act_ref[:, lo:lo + FF_TILE] = (_silu(gate) * up).astype(BF16)
    return x + 0.5 * _dot(act_ref[...], wout_ref[...])


def _ffn1_proj_kernel(tiles_per_seq, x_ref, g1_ref, win_ref, wout_ref, gm_ref, wmix_ref, wab_ref, convw_ref,
                      x1_ref, qkv_ref, z_ref, ab_ref, att_ref, act_ref, pre_ref):
    i = pl.program_id(0)
    rows = x_ref.shape[0]
    d3 = 3 * D_GROUP

    @pl.when(i == 0)
    def _():
        pre_ref[...] = jnp.zeros_like(pre_ref)

    @pl.when((i + tiles_per_seq - 1) % tiles_per_seq == 0)
    def _():
        pre_ref[0:CONV_HALO, :] = jnp.zeros((CONV_HALO, d3), F32)

    for t in range(3):
        lo = t * D_GROUP
        xp = pre_ref[:, lo:lo + D_GROUP]
        acc = convw_ref[0:1, lo:lo + D_GROUP] * xp
        for j in range(1, CONV_WIDTH):
            acc = pltpu.roll(acc, 1, axis=0) + convw_ref[j:j + 1, lo:lo + D_GROUP] * xp
        qkv_ref[:, lo:lo + D_GROUP] = _silu(acc[CONV_HALO:CONV_HALO + rows]).astype(BF16)
    pre_ref[0:CONV_HALO, :] = pre_ref[rows:rows + CONV_HALO, :]

    x1 = _ffn_half_step(x_ref[...], g1_ref, win_ref, wout_ref, act_ref)
    x1_ref[...] = x1
    h = _rmsnorm(x1, gm_ref[...]).astype(BF16)
    for t in range(3):
        lo = t * D_GROUP
        pre_ref[CONV_HALO:CONV_HALO + rows, lo:lo + D_GROUP] = _dot(h, wmix_ref[:, lo:lo + D_GROUP])
    z_ref[...] = _dot(h, wmix_ref[:, d3:d3 + D_GROUP]).astype(BF16)
    ab_ref[...] = _dot(h, wab_ref[...])
    for t in range(3):
        lo = t * D_GROUP
        r = _dot(h, wmix_ref[:, 4 * D_GROUP + lo:4 * D_GROUP + lo + D_GROUP])
        if t == 0:
            r = r * (HEAD_DIM ** -0.5 * LOG2E)
        att_ref[:, lo:lo + D_GROUP] = r.astype(BF16)


def _out_ffn2_kernel(x1_ref, yg_ref, ya_ref, wo_ref, g2_ref, win_ref, wout_ref, gf_ref,
                     o_ref, act_ref):
    x2 = (x1_ref[...] + _dot(yg_ref[...], wo_ref[0:D_GROUP, :])
          + _dot(ya_ref[...], wo_ref[D_GROUP:2 * D_GROUP, :]))
    x3 = _ffn_half_step(x2, g2_ref, win_ref, wout_ref, act_ref)
    o_ref[...] = _rmsnorm(x3, gf_ref[...])


def _attention_tiles(q_ref, bias_ref, o_ref, windows):
    first_head = lax.broadcasted_iota(jnp.int32, (1, LANES), 1) < HEAD_DIM
    units = [(t, p) for t in range(len(windows)) for p in range(N_PAIRS)]

    def scores(t, p):
        load_k, _, n_keys = windows[t]
        lo = p * LANES
        q2 = q_ref[t * ATT_TQ:(t + 1) * ATT_TQ, lo:lo + LANES]
        qs = jnp.concatenate(_by_head(first_head, q2), axis=0)
        s = _dot_nt(qs, load_k(lo))
        cols = []
        for w in range((ATT_WIN - n_keys) // LANES, ATT_WIN // LANES):
            c0 = w * LANES - (ATT_WIN - n_keys)
            sw = s[:, c0:c0 + LANES]
            if w not in ATT_BIAS_FREE_TILES:
                sw = sw + bias_ref[p, :, w * LANES:(w + 1) * LANES]
            cols.append(sw)
        return jnp.concatenate(cols, axis=1)

    s_next = scores(*units[0])
    for idx, (t, p) in enumerate(units):
        s = s_next
        if idx + 1 < len(units):
            s_next = scores(*units[idx + 1])
        lo = p * LANES
        m = jnp.max(s, axis=-1, keepdims=True)
        e = jnp.exp2(s - m)
        denom = jnp.sum(e, axis=-1, keepdims=True)
        pv = _dot(e.astype(BF16), windows[t][1](lo)) * (1.0 / denom)
        o_ref[t * ATT_TQ:(t + 1) * ATT_TQ, lo:lo + LANES] = jnp.where(
            first_head, pv[0:ATT_TQ], pv[ATT_TQ:2 * ATT_TQ]).astype(BF16)


def _attn_kernel(q_ref, kp_ref, kc_ref, vp_ref, vc_ref, bias_ref, o_ref):
    step = pl.program_id(1)

    @pl.when(step == 0)
    def _():
        windows = [((lambda lo, n=(t + 1) * ATT_TQ: kc_ref[0:n, lo:lo + LANES]),
                    (lambda lo, n=(t + 1) * ATT_TQ: vc_ref[0:n, lo:lo + LANES]),
                    (t + 1) * ATT_TQ) for t in range(ATT_STEP_TILES)]
        _attention_tiles(q_ref, bias_ref, o_ref, windows)

    @pl.when(step > 0)
    def _():
        def window(prev_ref, cur_ref, t):
            return lambda lo: jnp.concatenate([prev_ref[t * ATT_TQ:ATT_BACK, lo:lo + LANES],
                                               cur_ref[0:(t + 1) * ATT_TQ, lo:lo + LANES]], axis=0)
        windows = [(window(kp_ref, kc_ref, t), window(vp_ref, vc_ref, t), ATT_WIN) for t in range(ATT_STEP_TILES)]
        _attention_tiles(q_ref, bias_ref, o_ref, windows)


def _block_diag2(a, b):
    z = jnp.zeros_like(a)
    return jnp.concatenate([jnp.concatenate([a, z], axis=1), jnp.concatenate([z, b], axis=1)], axis=0)


def _inverse_stages(chains):
    n = GDN_BLOCK
    row = lax.broadcasted_iota(jnp.int32, (n, 2 * n), 0)
    col = lax.broadcasted_iota(jnp.int32, (n, 2 * n), 1) % n

    def init():
        eye = (row == col).astype(F32)
        base = (row // 2 == col // 2)
        for c in chains:
            c["t2"] = (eye - jnp.where(base, c["l2"], 0.0)).astype(BF16)
            c["lb"] = c["l2"].astype(BF16)

    def level(b):
        halves = b % (2 * SUBLANES) == 0
        segs = [(s, s + b) for s in range(b, n, 2 * b)]

        def take(a):
            return a if not halves else jnp.concatenate([a[s:e] for s, e in segs], axis=0)

        def put(full, part):
            if not halves:
                return part
            out, at, r = [], 0, 0
            for s, e in segs:
                out += [full[at:s], part[r:r + e - s]]
                at, r = e, r + e - s
            return jnp.concatenate(out, axis=0)

        def cx():
            cmask = ((row // (2 * b) == col // (2 * b)) & (row % (2 * b) >= b) & (col % (2 * b) < b)).astype(BF16)
            for c in chains:
                x = c["t2"]
                y = _dot(take(c["lb"] * cmask), _block_diag2(x[:, 0:n], x[:, n:2 * n])).astype(BF16)
                c["y"] = put(jnp.zeros((n, 2 * n), BF16), y)

        def xcx():
            for c in chains:
                x, y = c["t2"], c.pop("y")
                z = _dot(take(x), _block_diag2(y[:, 0:n], y[:, n:2 * n]))
                c["t2"] = put(x, take(x) - z.astype(BF16))

        return [cx, xcx]

    stages, b = [init], 2
    while b < n:
        stages += level(b)
        b *= 2
    return stages


def _gdn_stages(block_ids, qkv_ref, z_ref, ab_ref, arow_ref, dtrow_ref, gnorm_ref, tri_ref, ea_ref, eb_ref,
                ones_ref, y_ref, state_ref):
    n = GDN_BLOCK
    row2 = lax.broadcasted_iota(jnp.int32, (n, 2 * n), 0)
    col2 = lax.broadcasted_iota(jnp.int32, (n, 2 * n), 1) % n
    first_head = lax.broadcasted_iota(jnp.int32, (1, LANES), 1) < HEAD_DIM
    chains = []

    def gating():
        for b in block_ids:
            r0 = b * n
            ab = ab_ref[r0:r0 + n, :]
            g_n = -arow_ref[...] * _softplus(ab + dtrow_ref[...])
            beta_n = 1.0 / (1.0 + jnp.exp(-ab))
            gc_n = _select_rows(tri_ref[...], g_n)
            gc_x = _select_cols(gc_n, ea_ref[...])
            g_last = gc_x[n - 1:n, :]
            blk = dict(r0=r0, gc_t=gc_n.T, gc_x=gc_x, beta_x=_select_cols(beta_n, eb_ref[...], n_pieces=2),
                       exp_gc=jnp.exp(gc_x), exp_rest=jnp.exp(g_last - gc_x), exp_last=jnp.exp(g_last))
            for p in range(N_PAIRS):
                chains.append(dict(blk=blk, p=p, lo=p * LANES))

    def squares():
        ones_bd = ones_ref[...]
        for c in chains:
            lo, r0 = c["lo"], c["blk"]["r0"]
            c["q"] = qkv_ref[r0:r0 + n, lo:lo + LANES].astype(F32)
            c["k"] = qkv_ref[r0:r0 + n, D_GROUP + lo:D_GROUP + lo + LANES].astype(F32)
            c["sq"] = _dot(jnp.concatenate([c["q"] * c["q"], c["k"] * c["k"]], axis=0).astype(BF16), ones_bd)

    def token_products():
        for c in chains:
            blk, lo, r0 = c["blk"], c["lo"], c["blk"]["r0"]
            sq = c.pop("sq")
            q = c.pop("q") * lax.rsqrt(sq[0:n] + L2_EPS) * (HEAD_DIM ** -0.5)
            k = c.pop("k") * lax.rsqrt(sq[n:2 * n] + L2_EPS)
            v = qkv_ref[r0:r0 + n, 2 * D_GROUP + lo:2 * D_GROUP + lo + LANES].astype(F32)
            beta = blk["beta_x"][:, lo:lo + LANES]
            kb = k * beta
            c["vb"] = (v * beta).astype(BF16)
            c["kw"] = (kb * blk["exp_gc"][:, lo:lo + LANES]).astype(BF16)
            c["q_dec"] = (q * blk["exp_gc"][:, lo:lo + LANES]).astype(BF16)
            c["k_rest"] = (k * blk["exp_rest"][:, lo:lo + LANES]).astype(BF16)
            kf = k.astype(BF16)
            c["tt"] = _dot_nt(jnp.concatenate([kb.astype(BF16), q.astype(BF16)], axis=0),
                              jnp.concatenate(_by_head(first_head, kf), axis=0))

    def decays():
        lower2 = row2 >= col2
        strict2 = row2 > col2
        for c in chains:
            blk, p, lo = c["blk"], c["p"], c["lo"]
            g2 = blk["gc_x"][:, lo:lo + LANES]
            g2_swapped = pltpu.roll(g2, HEAD_DIM, axis=1)
            gcol = jnp.concatenate([jnp.where(first_head, g2, g2_swapped),
                                    jnp.where(first_head, g2_swapped, g2)], axis=1)
            grow = jnp.concatenate([jnp.broadcast_to(blk["gc_t"][2 * p:2 * p + 1, :], (n, n)),
                                    jnp.broadcast_to(blk["gc_t"][2 * p + 1:2 * p + 2, :], (n, n))], axis=1)
            dec = jnp.exp(jnp.minimum(gcol - grow, 0.0))
            tt = c.pop("tt")
            c["l2"] = jnp.where(strict2, tt[0:n] * dec, 0.0)
            c["aqk"] = jnp.where(lower2, tt[n:2 * n] * dec, 0.0).astype(BF16)

    def solve():
        for c in chains:
            vb0, vb1 = _by_head(first_head, c.pop("vb"))
            kw0, kw1 = _by_head(first_head, c.pop("kw"))
            rhs = jnp.concatenate([jnp.concatenate([vb0, kw0], axis=1),
                                   jnp.concatenate([vb1, kw1], axis=1)], axis=0)
            c["uw"] = _dot(c.pop("t2"), rhs)
            c.pop("l2"), c.pop("lb")

    def state_in(i):
        def run():
            for c in chains[i * N_PAIRS:(i + 1) * N_PAIRS]:
                c["state"] = state_ref[c["p"]]
                w = c["uw"][:, LANES:2 * LANES].astype(BF16)
                c["ws"] = _dot(jnp.concatenate([w, c.pop("q_dec")], axis=0), c["state"].astype(BF16))
        return run

    def state_out(i):
        def run():
            head_block = (lax.broadcasted_iota(jnp.int32, (LANES, LANES), 0) // HEAD_DIM
                          == lax.broadcasted_iota(jnp.int32, (LANES, LANES), 1) // HEAD_DIM)
            for c in chains[i * N_PAIRS:(i + 1) * N_PAIRS]:
                lo = c["lo"]
                ws = c.pop("ws")
                v_new = (c.pop("uw")[:, 0:LANES] - ws[0:n]).astype(BF16)
                upd = _dot_tn(c.pop("k_rest"), v_new)
                intra = _dot(c.pop("aqk"), jnp.concatenate(_by_head(first_head, v_new), axis=0))
                state_ref[c["p"]] = (c.pop("state") * c["blk"]["exp_last"][:, lo:lo + LANES]
                                     + jnp.where(head_block, upd, 0.0))
                c["o"] = ws[n:2 * n] + intra
        return run

    def mean_squares():
        ones_bd = ones_ref[...]
        for c in chains:
            c["ms"] = _dot((c["o"] * c["o"]).astype(BF16), ones_bd) * (1.0 / HEAD_DIM)

    def outputs():
        for c in chains:
            lo, r0 = c["lo"], c["blk"]["r0"]
            z = z_ref[r0:r0 + n, lo:lo + LANES].astype(F32)
            y = c.pop("o") * lax.rsqrt(c.pop("ms") + RMS_EPS) * gnorm_ref[:, lo:lo + LANES] * _silu(z)
            y_ref[r0:r0 + n, lo:lo + LANES] = y.astype(BF16)

    prep = [gating, squares, token_products, decays]
    tail = [solve]
    for i in range(len(block_ids)):
        tail += [state_in(i), state_out(i)]
    return prep, _inverse_stages(chains), tail + [mean_squares, outputs]


def _interleave(a, b):
    out, ia, ib = [], 0, 0
    while ia < len(a) or ib < len(b):
        if ib >= len(b) or (ia < len(a) and ia * len(b) <= ib * len(a)):
            out.append(a[ia])
            ia += 1
        else:
            out.append(b[ib])
            ib += 1
    return out


def _gdn_kernel(qkv_ref, z_ref, ab_ref, arow_ref, dtrow_ref, gnorm_ref, tri_ref, ea_ref, eb_ref, ones_ref,
                y_ref, state_ref):
    @pl.when(pl.program_id(1) == 0)
    def _():
        state_ref[...] = jnp.zeros_like(state_ref)

    refs = (qkv_ref, z_ref, ab_ref, arow_ref, dtrow_ref, gnorm_ref, tri_ref, ea_ref, eb_ref, ones_ref,
            y_ref, state_ref)
    n_groups = GDN_STEP_BLOCKS // GDN_GROUP_BLOCKS
    groups = [_gdn_stages(tuple(range(g * GDN_GROUP_BLOCKS, (g + 1) * GDN_GROUP_BLOCKS)), *refs)
              for g in range(n_groups)]
    order = list(groups[0][0])
    for g in range(n_groups):
        beside = list(groups[g - 1][2]) if g > 0 else []
        if g + 1 < n_groups:
            beside = _interleave(beside, groups[g + 1][0]) if beside else list(groups[g + 1][0])
        order += _interleave(groups[g][1], beside)
    order += groups[-1][2]
    for stage in order:
        stage()


def _resident(shape):
    zeros = (0,) * len(shape)
    return pl.BlockSpec(shape, lambda *_: zeros, pipeline_mode=pl.Buffered(1))


def _attention_bias(rel_bias):
    span = ATT_WIN + ATT_TQ - 1
    rel = (ATT_WIN - 1) - jnp.arange(span)
    u = rel_bias.astype(F32)[:, jnp.clip(rel, -REL_CLIP, REL_CLIP) + REL_CLIP]
    u = jnp.pad(u, ((0, 0), (0, 1)))
    skew = jnp.tile(u, (1, ATT_TQ))[:, :ATT_TQ * span].reshape(N_HEADS, ATT_TQ, span)
    bias = skew[:, :, ATT_TQ - 1:ATT_TQ - 1 + ATT_WIN]
    bias = (bias - rel_bias.astype(F32)[:, 2 * REL_CLIP][:, None, None]) * LOG2E
    qpos = jnp.arange(ATT_TQ)[:, None]
    kpos = jnp.arange(ATT_WIN)[None, :] - ATT_BACK
    dchunk = qpos // ATT_CHUNK - kpos // ATT_CHUNK
    valid = (dchunk >= 0) & (dchunk <= BAND_CHUNKS)
    return jnp.where(valid[None], bias, NEG_INF).reshape(N_PAIRS, 2 * ATT_TQ, ATT_WIN)


def _layer(x, ffn1_norm, ffn1_w_in, ffn1_w_out, mix_norm, w_in_mix, conv_w, A_log, dt_bias,
           gdn_norm, rel_bias, w_out_mix, ffn2_norm, ffn2_w_in, ffn2_w_out, final_norm):
    bsz, seq, _ = x.shape
    n_tok = bsz * seq
    gdn_rows = GDN_BLOCK * GDN_STEP_BLOCKS
    att_rows = ATT_TQ * ATT_STEP_TILES
    assert seq % ROW_TILE == 0 and seq % gdn_rows == 0 and seq % att_rows == 0 and n_tok % OUT_ROW_TILE == 0
    n_row_tiles = n_tok // ROW_TILE
    row = lambda i: (i, 0)
    params = functools.partial(pltpu.CompilerParams, vmem_limit_bytes=VMEM_LIMIT)

    gq = 3 * D_GROUP
    w_mix = w_in_mix.astype(BF16)
    w_main = jnp.concatenate([w_mix[:, 0:4 * D_GROUP], w_mix[:, 4 * D_GROUP + 2 * N_HEADS:]], axis=1)
    w_ab = jnp.pad(w_mix[:, 4 * D_GROUP:4 * D_GROUP + 2 * N_HEADS], ((0, 0), (0, LANES - 2 * N_HEADS)))

    cur = lambda i: (jnp.minimum(i, n_row_tiles - 1), 0)
    prev = lambda i: (jnp.maximum(i - 1, 0), 0)
    x1, gdn_qkv, gdn_z, ab, att_in = pl.pallas_call(
        functools.partial(_ffn1_proj_kernel, seq // ROW_TILE),
        grid=(n_row_tiles + 1,),
        in_specs=[pl.BlockSpec((ROW_TILE, D_MODEL), cur),
                  _resident((1, D_MODEL)), _resident((D_MODEL, 2 * D_FF)), _resident((D_FF, D_MODEL)),
                  _resident((1, D_MODEL)), _resident((D_MODEL, 7 * D_GROUP)), _resident((D_MODEL, LANES)),
                  _resident((CONV_WIDTH, gq))],
        out_specs=[pl.BlockSpec((ROW_TILE, D_MODEL), cur), pl.BlockSpec((ROW_TILE, gq), prev),
                   pl.BlockSpec((ROW_TILE, D_GROUP), cur), pl.BlockSpec((ROW_TILE, LANES), cur),
                   pl.BlockSpec((ROW_TILE, gq), cur)],
        out_shape=[jax.ShapeDtypeStruct((n_tok, D_MODEL), F32), jax.ShapeDtypeStruct((n_tok, gq), BF16),
                   jax.ShapeDtypeStruct((n_tok, D_GROUP), BF16), jax.ShapeDtypeStruct((n_tok, LANES), F32),
                   jax.ShapeDtypeStruct((n_tok, gq), BF16)],
        scratch_shapes=[pltpu.VMEM((ROW_TILE, D_FF), BF16), pltpu.VMEM((ROW_TILE + CONV_HALO, gq), F32)],
        compiler_params=params(dimension_semantics=("arbitrary",)),
        name="ffn1_proj",
    )(x.reshape(n_tok, D_MODEL), ffn1_norm.reshape(1, D_MODEL).astype(F32),
      ffn1_w_in.astype(BF16), ffn1_w_out.astype(BF16),
      mix_norm.reshape(1, D_MODEL).astype(F32), w_main, w_ab, conv_w.astype(F32))

    n = GDN_BLOCK
    head_of_lane = jnp.arange(D_GROUP) // HEAD_DIM
    lane = jnp.arange(LANES)
    tri = (jnp.arange(n)[:, None] >= jnp.arange(n)[None, :]).astype(BF16)
    e_a = (lane[:, None] == head_of_lane[None, :]).astype(BF16)
    e_b = (lane[:, None] == (head_of_lane + N_HEADS)[None, :]).astype(BF16)
    ones_bd = (lane[:, None] // HEAD_DIM == lane[None, :] // HEAD_DIM).astype(BF16)
    a_row = jnp.pad(jnp.exp(A_log.astype(F32)), (0, LANES - N_HEADS)).reshape(1, LANES)
    dt_row = jnp.pad(dt_bias.astype(F32), (0, LANES - N_HEADS)).reshape(1, LANES)
    gnorm_row = jnp.tile(gdn_norm.astype(F32), N_HEADS).reshape(1, D_GROUP)
    blk3 = lambda b, i: (b, i, 0)

    y_gdn = pl.pallas_call(
        _gdn_kernel,
        grid=(bsz, seq // gdn_rows),
        in_specs=[pl.BlockSpec((None, gdn_rows, gq), blk3), pl.BlockSpec((None, gdn_rows, D_GROUP), blk3),
                  pl.BlockSpec((None, gdn_rows, LANES), blk3),
                  _resident((1, LANES)), _resident((1, LANES)), _resident((1, D_GROUP)), _resident((n, n)),
                  _resident((LANES, D_GROUP)), _resident((LANES, D_GROUP)), _resident((LANES, LANES))],
        out_specs=pl.BlockSpec((None, gdn_rows, D_GROUP), blk3),
        out_shape=jax.ShapeDtypeStruct((bsz, seq, D_GROUP), BF16),
        scratch_shapes=[pltpu.VMEM((N_PAIRS, LANES, LANES), F32)],
        compiler_params=params(dimension_semantics=("arbitrary", "arbitrary")),
        name="gated_deltanet",
    )(gdn_qkv.reshape(bsz, seq, gq), gdn_z.reshape(bsz, seq, D_GROUP), ab.reshape(bsz, seq, LANES),
      a_row, dt_row, gnorm_row, tri, e_a, e_b, ones_bd)

    att3 = att_in.reshape(bsz, seq, gq)
    y_att = pl.pallas_call(
        _attn_kernel,
        grid=(bsz, seq // att_rows),
        in_specs=[pl.BlockSpec((None, att_rows, D_GROUP), lambda b, i: (b, i, 0)),
                  pl.BlockSpec((None, att_rows, D_GROUP), lambda b, i: (b, jnp.maximum(i - 1, 0), 1)),
                  pl.BlockSpec((None, att_rows, D_GROUP), lambda b, i: (b, i, 1)),
                  pl.BlockSpec((None, att_rows, D_GROUP), lambda b, i: (b, jnp.maximum(i - 1, 0), 2)),
                  pl.BlockSpec((None, att_rows, D_GROUP), lambda b, i: (b, i, 2)),
                  _resident((N_PAIRS, 2 * ATT_TQ, ATT_WIN))],
        out_specs=pl.BlockSpec((None, att_rows, D_GROUP), blk3),
        out_shape=jax.ShapeDtypeStruct((bsz, seq, D_GROUP), BF16),
        compiler_params=params(dimension_semantics=("arbitrary", "arbitrary")),
        name="chunk_attention",
    )(att3, att3, att3, att3, att3, _attention_bias(rel_bias))

    out = pl.pallas_call(
        _out_ffn2_kernel,
        grid=(n_tok // OUT_ROW_TILE,),
        in_specs=[pl.BlockSpec((OUT_ROW_TILE, D_MODEL), row), pl.BlockSpec((OUT_ROW_TILE, D_GROUP), row),
                  pl.BlockSpec((OUT_ROW_TILE, D_GROUP), row), _resident((2 * D_GROUP, D_MODEL)),
                  _resident((1, D_MODEL)), _resident((D_MODEL, 2 * D_FF)), _resident((D_FF, D_MODEL)),
                  _resident((1, D_MODEL))],
        out_specs=pl.BlockSpec((OUT_ROW_TILE, D_MODEL), row),
        out_shape=jax.ShapeDtypeStruct((n_tok, D_MODEL), F32),
        scratch_shapes=[pltpu.VMEM((OUT_ROW_TILE, D_FF), BF16)],
        compiler_params=params(dimension_semantics=("arbitrary",)),
        name="out_ffn2_norm",
    )(x1, y_gdn.reshape(n_tok, D_GROUP), y_att.reshape(n_tok, D_GROUP), w_out_mix.astype(BF16),
      ffn2_norm.reshape(1, D_MODEL).astype(F32), ffn2_w_in.astype(BF16), ffn2_w_out.astype(BF16),
      final_norm.reshape(1, D_MODEL).astype(F32))
    return out.reshape(bsz, seq, D_MODEL)


def kernel(x, ffn1_norm, ffn1_w_in, ffn1_w_out, mix_norm, w_in_mix, conv_w, A_log, dt_bias, gdn_norm,
           rel_bias, w_out_mix, ffn2_norm, ffn2_w_in, ffn2_w_out, final_norm):
    assert ffn1_norm.shape[0] == 1, "the final norm is fused into the single layer's last call"
    return _layer(x, ffn1_norm[0], ffn1_w_in[0], ffn1_w_out[0], mix_norm[0], w_in_mix[0], conv_w[0],
                  A_log[0], dt_bias[0], gdn_norm[0], rel_bias[0], w_out_mix[0], ffn2_norm[0],
                  ffn2_w_in[0], ffn2_w_out[0], final_norm)
```

```python
import functools

import jax
import jax.numpy as jnp
from jax import lax
from jax.experimental import pallas as pl
from jax.experimental.pallas import tpu as pltpu

F32 = jnp.float32
BF16 = jnp.bfloat16

D_MODEL = 1024
HEAD_DIM = 64
N_HEADS = 8
D_GROUP = N_HEADS * HEAD_DIM
D_FF = 2816
CONV_WIDTH = 4
ATT_CHUNK = 64
BAND_CHUNKS = 8
REL_CLIP = 128
RMS_EPS = 1e-6
L2_EPS = 1e-6
NEG_INF = -1e30

LANES = 128
SUBLANES = 8
N_PAIRS = D_GROUP // LANES
MXU_COLS = 256
FF_TILE = MXU_COLS
N_FF_TILES = D_FF // FF_TILE
ROW_TILE = 512
OUT_ROW_TILE = 1024
GDN_BLOCK = 128
GDN_STEP_BLOCKS = 8
GDN_GROUP_BLOCKS = 2
ATT_TQ = 128
ATT_BACK = BAND_CHUNKS * ATT_CHUNK
ATT_WIN = ATT_BACK + ATT_TQ
ATT_STEP_TILES = ATT_BACK // ATT_TQ
ATT_SCORES_AHEAD = 2
LOG2E = 1.4426950408889634


def _bias_free_tiles():
    free = []
    for w in range(ATT_WIN // LANES):
        k_lo, k_hi = w * LANES - ATT_BACK, w * LANES - ATT_BACK + LANES - 1
        clipped = 0 - k_hi >= REL_CLIP
        dchunks = [qc - kc for qc in range(ATT_TQ // ATT_CHUNK) for kc in (k_lo // ATT_CHUNK, k_hi // ATT_CHUNK)]
        if clipped and min(dchunks) >= 0 and max(dchunks) <= BAND_CHUNKS:
            free.append(w)
    return tuple(free)


ATT_BIAS_FREE_TILES = _bias_free_tiles()
CONV_HALO = SUBLANES
VMEM_LIMIT = 56 * 1024 * 1024


def _dot(a, b):
    return jnp.dot(a, b, preferred_element_type=F32)


def _dot_nt(a, b):
    return lax.dot_general(a, b, (((1,), (1,)), ((), ())), preferred_element_type=F32)


def _dot_tn(a, b):
    return lax.dot_general(a, b, (((0,), (0,)), ((), ())), preferred_element_type=F32)


def _bf16_pieces(x, n_pieces):
    pieces = []
    for _ in range(n_pieces - 1):
        hi = x.astype(BF16)
        pieces.append(hi)
        x = x - hi.astype(F32)
    pieces.append(x.astype(BF16))
    return pieces


def _select_cols(x, sel, n_pieces=3):
    return functools.reduce(lambda a, b: a + b, [_dot(p, sel) for p in _bf16_pieces(x, n_pieces)])


def _select_rows(sel, x, n_pieces=3):
    return functools.reduce(lambda a, b: a + b, [_dot(sel, p) for p in _bf16_pieces(x, n_pieces)])


def _rmsnorm(x, gain):
    ms = jnp.mean(x * x, axis=-1, keepdims=True)
    return x * lax.rsqrt(ms + RMS_EPS) * gain


def _silu(x):
    return x * (1.0 / (1.0 + jnp.exp(-x)))


def _softplus(x):
    return jnp.maximum(x, 0.0) + jnp.log(1.0 + jnp.exp(-jnp.abs(x)))


def _by_head(first_head, x):
    z = jnp.zeros_like(x)
    return jnp.where(first_head, x, z), jnp.where(first_head, z, x)


def _ffn_half_step(x, gain_ref, win_ref, wout_ref, act_ref):
    h = _rmsnorm(x, gain_ref[...]).astype(BF16)
    for j in range(N_FF_TILES):
        lo = j * FF_TILE
        gate = _dot(h, win_ref[:, lo:lo + FF_TILE])
        up = _dot(h, win_ref[:, D_FF + lo:D_FF + lo + FF_TILE])
        act_ref[:, lo:lo + FF_TILE] = (_silu(gate) * up).astype(BF16)
    return x + 0.5 * _dot(act_ref[...], wout_ref[...])


def _ffn1_proj_kernel(tiles_per_seq, x_ref, g1_ref, win_ref, wout_ref, gm_ref, wmix_ref, wab_ref, convw_ref,
                      x1_ref, qkv_ref, z_ref, ab_ref, att_ref, act_ref, pre_ref):
    i = pl.program_id(0)
    rows = x_ref.shape[0]
    d3 = 3 * D_GROUP

    @pl.when(i == 0)
    def _():
        pre_ref[...] = jnp.zeros_like(pre_ref)

    @pl.when((i + tiles_per_seq - 1) % tiles_per_seq == 0)
    def _():
        pre_ref[0:CONV_HALO, :] = jnp.zeros((CONV_HALO, d3), F32)

    for t in range(3):
        lo = t * D_GROUP
        xp = pre_ref[:, lo:lo + D_GROUP]
        acc = convw_ref[0:1, lo:lo + D_GROUP] * xp
        for j in range(1, CONV_WIDTH):
            acc = pltpu.roll(acc, 1, axis=0) + convw_ref[j:j + 1, lo:lo + D_GROUP] * xp
        qkv_ref[:, lo:lo + D_GROUP] = _silu(acc[CONV_HALO:CONV_HALO + rows]).astype(BF16)
    pre_ref[0:CONV_HALO, :] = pre_ref[rows:rows + CONV_HALO, :]

    x1 = _ffn_half_step(x_ref[...], g1_ref, win_ref, wout_ref, act_ref)
    x1_ref[...] = x1
    h = _rmsnorm(x1, gm_ref[...]).astype(BF16)
    for t in range(3):
        lo = t * D_GROUP
        pre_ref[CONV_HALO:CONV_HALO + rows, lo:lo + D_GROUP] = _dot(h, wmix_ref[:, lo:lo + D_GROUP])
    z_ref[...] = _dot(h, wmix_ref[:, d3:d3 + D_GROUP]).astype(BF16)
    ab_ref[...] = _dot(h, wab_ref[...])
    for t in range(3):
        lo = t * D_GROUP
        r = _dot(h, wmix_ref[:, 4 * D_GROUP + lo:4 * D_GROUP + lo + D_GROUP])
        if t == 0:
            r = r * (HEAD_DIM ** -0.5 * LOG2E)
        att_ref[:, lo:lo + D_GROUP] = r.astype(BF16)


def _out_ffn2_kernel(x1_ref, yg_ref, ya_ref, wo_ref, g2_ref, win_ref, wout_ref, gf_ref,
                     o_ref, act_ref):
    x2 = (x1_ref[...] + _dot(yg_ref[...], wo_ref[0:D_GROUP, :])
          + _dot(ya_ref[...], wo_ref[D_GROUP:2 * D_GROUP, :]))
    x3 = _ffn_half_step(x2, g2_ref, win_ref, wout_ref, act_ref)
    o_ref[...] = _rmsnorm(x3, gf_ref[...])


def _attention_tiles(q_ref, bias_ref, o_ref, windows):
    first_head = lax.broadcasted_iota(jnp.int32, (1, LANES), 1) < HEAD_DIM
    units = [(t, p) for t in range(len(windows)) for p in range(N_PAIRS)]

    def scores(t, p):
        load_k, _, n_keys = windows[t]
        lo = p * LANES
        q2 = q_ref[t * ATT_TQ:(t + 1) * ATT_TQ, lo:lo + LANES]
        qs = jnp.concatenate(_by_head(first_head, q2), axis=0)
        s = _dot_nt(qs, load_k(lo))
        cols = []
        for w in range((ATT_WIN - n_keys) // LANES, ATT_WIN // LANES):
            c0 = w * LANES - (ATT_WIN - n_keys)
            sw = s[:, c0:c0 + LANES]
            if w not in ATT_BIAS_FREE_TILES:
                sw = sw + bias_ref[p, :, w * LANES:(w + 1) * LANES]
            cols.append(sw)
        return jnp.concatenate(cols, axis=1)

    pending = [scores(*u) for u in units[:ATT_SCORES_AHEAD]]
    for idx, (t, p) in enumerate(units):
        s = pending.pop(0)
        if idx + ATT_SCORES_AHEAD < len(units):
            pending.append(scores(*units[idx + ATT_SCORES_AHEAD]))
        lo = p * LANES
        m = jnp.max(s, axis=-1, keepdims=True)
        e = jnp.exp2(s - m)
        denom = jnp.sum(e, axis=-1, keepdims=True)
        pv = _dot(e.astype(BF16), windows[t][1](lo)) * (1.0 / denom)
        o_ref[t * ATT_TQ:(t + 1) * ATT_TQ, lo:lo + LANES] = jnp.where(
            first_head, pv[0:ATT_TQ], pv[ATT_TQ:2 * ATT_TQ]).astype(BF16)


def _attn_kernel(q_ref, kp_ref, kc_ref, vp_ref, vc_ref, bias_ref, o_ref):
    step = pl.program_id(1)

    @pl.when(step == 0)
    def _():
        windows = [((lambda lo, n=(t + 1) * ATT_TQ: kc_ref[0:n, lo:lo + LANES]),
                    (lambda lo, n=(t + 1) * ATT_TQ: vc_ref[0:n, lo:lo + LANES]),
                    (t + 1) * ATT_TQ) for t in range(ATT_STEP_TILES)]
        _attention_tiles(q_ref, bias_ref, o_ref, windows)

    @pl.when(step > 0)
    def _():
        def window(prev_ref, cur_ref, t):
            return lambda lo: jnp.concatenate([prev_ref[t * ATT_TQ:ATT_BACK, lo:lo + LANES],
                                               cur_ref[0:(t + 1) * ATT_TQ, lo:lo + LANES]], axis=0)
        windows = [(window(kp_ref, kc_ref, t), window(vp_ref, vc_ref, t), ATT_WIN) for t in range(ATT_STEP_TILES)]
        _attention_tiles(q_ref, bias_ref, o_ref, windows)


def _block_diag2(a, b):
    z = jnp.zeros_like(a)
    return jnp.concatenate([jnp.concatenate([a, z], axis=1), jnp.concatenate([z, b], axis=1)], axis=0)


def _inverse_stages(chains):
    n = GDN_BLOCK
    row = lax.broadcasted_iota(jnp.int32, (n, 2 * n), 0)
    col = lax.broadcasted_iota(jnp.int32, (n, 2 * n), 1) % n

    def init():
        eye = (row == col).astype(F32)
        base = (row // 2 == col // 2)
        for c in chains:
            c["t2"] = (eye - jnp.where(base, c["l2"], 0.0)).astype(BF16)
            c["lb"] = c["l2"].astype(BF16)

    def level(b):
        halves = b % (2 * SUBLANES) == 0
        segs = [(s, s + b) for s in range(b, n, 2 * b)]

        def take(a):
            return a if not halves else jnp.concatenate([a[s:e] for s, e in segs], axis=0)

        def put(full, part):
            if not halves:
                return part
            out, at, r = [], 0, 0
            for s, e in segs:
                out += [full[at:s], part[r:r + e - s]]
                at, r = e, r + e - s
            return jnp.concatenate(out, axis=0)

        def cx():
            cmask = ((row // (2 * b) == col // (2 * b)) & (row % (2 * b) >= b) & (col % (2 * b) < b)).astype(BF16)
            for c in chains:
                x = c["t2"]
                y = _dot(take(c["lb"] * cmask), _block_diag2(x[:, 0:n], x[:, n:2 * n])).astype(BF16)
                c["y"] = put(jnp.zeros((n, 2 * n), BF16), y)

        def xcx():
            for c in chains:
                x, y = c["t2"], c.pop("y")
                z = _dot(take(x), _block_diag2(y[:, 0:n], y[:, n:2 * n]))
                c["t2"] = put(x, take(x) - z.astype(BF16))

        return [cx, xcx]

    stages, b = [init], 2
    while b < n:
        stages += level(b)
        b *= 2
    return stages


def _gdn_stages(block_ids, qkv_ref, z_ref, ab_ref, arow_ref, dtrow_ref, gnorm_ref, tri_ref, ea_ref, eb_ref,
                ones_ref, y_ref, state_ref):
    n = GDN_BLOCK
    row2 = lax.broadcasted_iota(jnp.int32, (n, 2 * n), 0)
    col2 = lax.broadcasted_iota(jnp.int32, (n, 2 * n), 1) % n
    first_head = lax.broadcasted_iota(jnp.int32, (1, LANES), 1) < HEAD_DIM
    chains = []

    def gating():
        for b in block_ids:
            r0 = b * n
            ab = ab_ref[r0:r0 + n, :]
            g_n = -arow_ref[...] * _softplus(ab + dtrow_ref[...])
            beta_n = 1.0 / (1.0 + jnp.exp(-ab))
            gc_n = _select_rows(tri_ref[...], g_n)
            gc_x = _select_cols(gc_n, ea_ref[...])
            g_last = gc_x[n - 1:n, :]
            blk = dict(r0=r0, gc_t=gc_n.T, gc_x=gc_x, beta_x=_select_cols(beta_n, eb_ref[...], n_pieces=2),
                       exp_gc=jnp.exp(gc_x), exp_rest=jnp.exp(g_last - gc_x), exp_last=jnp.exp(g_last))
            for p in range(N_PAIRS):
                chains.append(dict(blk=blk, p=p, lo=p * LANES))

    def squares():
        ones_bd = ones_ref[...]
        for c in chains:
            lo, r0 = c["lo"], c["blk"]["r0"]
            c["q"] = qkv_ref[r0:r0 + n, lo:lo + LANES].astype(F32)
            c["k"] = qkv_ref[r0:r0 + n, D_GROUP + lo:D_GROUP + lo + LANES].astype(F32)
            c["sq"] = _dot(jnp.concatenate([c["q"] * c["q"], c["k"] * c["k"]], axis=0).astype(BF16), ones_bd)

    def token_products():
        for c in chains:
            blk, lo, r0 = c["blk"], c["lo"], c["blk"]["r0"]
            sq = c.pop("sq")
            q = c.pop("q") * lax.rsqrt(sq[0:n] + L2_EPS) * (HEAD_DIM ** -0.5)
            k = c.pop("k") * lax.rsqrt(sq[n:2 * n] + L2_EPS)
            v = qkv_ref[r0:r0 + n, 2 * D_GROUP + lo:2 * D_GROUP + lo + LANES].astype(F32)
            beta = blk["beta_x"][:, lo:lo + LANES]
            kb = k * beta
            c["vb"] = (v * beta).astype(BF16)
            c["kw"] = (kb * blk["exp_gc"][:, lo:lo + LANES]).astype(BF16)
            c["q_dec"] = (q * blk["exp_gc"][:, lo:lo + LANES]).astype(BF16)
            c["k_rest"] = (k * blk["exp_rest"][:, lo:lo + LANES]).astype(BF16)
            kf = k.astype(BF16)
            c["tt"] = _dot_nt(jnp.concatenate([kb.astype(BF16), q.astype(BF16)], axis=0),
                              jnp.concatenate(_by_head(first_head, kf), axis=0))

    def decays():
        lower2 = row2 >= col2
        strict2 = row2 > col2
        for c in chains:
            blk, p, lo = c["blk"], c["p"], c["lo"]
            g2 = blk["gc_x"][:, lo:lo + LANES]
            g2_swapped = pltpu.roll(g2, HEAD_DIM, axis=1)
            gcol = jnp.concatenate([jnp.where(first_head, g2, g2_swapped),
                                    jnp.where(first_head, g2_swapped, g2)], axis=1)
            grow = jnp.concatenate([jnp.broadcast_to(blk["gc_t"][2 * p:2 * p + 1, :], (n, n)),
                                    jnp.broadcast_to(blk["gc_t"][2 * p + 1:2 * p + 2, :], (n, n))], axis=1)
            dec = jnp.exp(jnp.minimum(gcol - grow, 0.0))
            tt = c.pop("tt")
            c["l2"] = jnp.where(strict2, tt[0:n] * dec, 0.0)
            c["aqk"] = jnp.where(lower2, tt[n:2 * n] * dec, 0.0).astype(BF16)

    def solve():
        for c in chains:
            vb0, vb1 = _by_head(first_head, c.pop("vb"))
            kw0, kw1 = _by_head(first_head, c.pop("kw"))
            rhs = jnp.concatenate([jnp.concatenate([vb0, kw0], axis=1),
                                   jnp.concatenate([vb1, kw1], axis=1)], axis=0)
            c["uw"] = _dot(c.pop("t2"), rhs)
            c.pop("l2"), c.pop("lb")

    def state_in(i):
        def run():
            for c in chains[i * N_PAIRS:(i + 1) * N_PAIRS]:
                c["state"] = state_ref[c["p"]]
                w = c["uw"][:, LANES:2 * LANES].astype(BF16)
                c["ws"] = _dot(jnp.concatenate([w, c.pop("q_dec")], axis=0), c["state"].astype(BF16))
        return run

    def state_out(i):
        def run():
            head_block = (lax.broadcasted_iota(jnp.int32, (LANES, LANES), 0) // HEAD_DIM
                          == lax.broadcasted_iota(jnp.int32, (LANES, LANES), 1) // HEAD_DIM)
            for c in chains[i * N_PAIRS:(i + 1) * N_PAIRS]:
                lo = c["lo"]
                ws = c.pop("ws")
                v_new = (c.pop("uw")[:, 0:LANES] - ws[0:n]).astype(BF16)
                upd = _dot_tn(c.pop("k_rest"), v_new)
                intra = _dot(c.pop("aqk"), jnp.concatenate(_by_head(first_head, v_new), axis=0))
                state_ref[c["p"]] = (c.pop("state") * c["blk"]["exp_last"][:, lo:lo + LANES]
                                     + jnp.where(head_block, upd, 0.0))
                c["o"] = ws[n:2 * n] + intra
        return run

    def mean_squares():
        ones_bd = ones_ref[...]
        for c in chains:
            c["ms"] = _dot((c["o"] * c["o"]).astype(BF16), ones_bd) * (1.0 / HEAD_DIM)

    def outputs():
        for c in chains:
            lo, r0 = c["lo"], c["blk"]["r0"]
            z = z_ref[r0:r0 + n, lo:lo + LANES].astype(F32)
            y = c.pop("o") * lax.rsqrt(c.pop("ms") + RMS_EPS) * gnorm_ref[:, lo:lo + LANES] * _silu(z)
            y_ref[r0:r0 + n, lo:lo + LANES] = y.astype(BF16)

    prep = [gating, squares, token_products, decays]
    tail = [solve]
    for i in range(len(block_ids)):
        tail += [state_in(i), state_out(i)]
    return prep, _inverse_stages(chains), tail + [mean_squares, outputs]


def _interleave(a, b):
    out, ia, ib = [], 0, 0
    while ia < len(a) or ib < len(b):
        if ib >= len(b) or (ia < len(a) and ia * len(b) <= ib * len(a)):
            out.append(a[ia])
            ia += 1
        else:
            out.append(b[ib])
            ib += 1
    return out


def _gdn_kernel(qkv_ref, z_ref, ab_ref, arow_ref, dtrow_ref, gnorm_ref, tri_ref, ea_ref, eb_ref, ones_ref,
                y_ref, state_ref):
    @pl.when(pl.program_id(1) == 0)
    def _():
        state_ref[...] = jnp.zeros_like(state_ref)

    refs = (qkv_ref, z_ref, ab_ref, arow_ref, dtrow_ref, gnorm_ref, tri_ref, ea_ref, eb_ref, ones_ref,
            y_ref, state_ref)
    n_groups = GDN_STEP_BLOCKS // GDN_GROUP_BLOCKS
    groups = [_gdn_stages(tuple(range(g * GDN_GROUP_BLOCKS, (g + 1) * GDN_GROUP_BLOCKS)), *refs)
              for g in range(n_groups)]
    order = list(groups[0][0])
    for g in range(n_groups):
        beside = list(groups[g - 1][2]) if g > 0 else []
        if g + 1 < n_groups:
            beside = _interleave(beside, groups[g + 1][0]) if beside else list(groups[g + 1][0])
        order += _interleave(groups[g][1], beside)
    order += groups[-1][2]
    for stage in order:
        stage()


def _resident(shape):
    zeros = (0,) * len(shape)
    return pl.BlockSpec(shape, lambda *_: zeros, pipeline_mode=pl.Buffered(1))


def _attention_bias(rel_bias):
    span = ATT_WIN + ATT_TQ - 1
    rel = (ATT_WIN - 1) - jnp.arange(span)
    u = rel_bias.astype(F32)[:, jnp.clip(rel, -REL_CLIP, REL_CLIP) + REL_CLIP]
    u = jnp.pad(u, ((0, 0), (0, 1)))
    skew = jnp.tile(u, (1, ATT_TQ))[:, :ATT_TQ * span].reshape(N_HEADS, ATT_TQ, span)
    bias = skew[:, :, ATT_TQ - 1:ATT_TQ - 1 + ATT_WIN]
    bias = (bias - rel_bias.astype(F32)[:, 2 * REL_CLIP][:, None, None]) * LOG2E
    qpos = jnp.arange(ATT_TQ)[:, None]
    kpos = jnp.arange(ATT_WIN)[None, :] - ATT_BACK
    dchunk = qpos // ATT_CHUNK - kpos // ATT_CHUNK
    valid = (dchunk >= 0) & (dchunk <= BAND_CHUNKS)
    return jnp.where(valid[None], bias, NEG_INF).reshape(N_PAIRS, 2 * ATT_TQ, ATT_WIN)


def _layer(x, ffn1_norm, ffn1_w_in, ffn1_w_out, mix_norm, w_in_mix, conv_w, A_log, dt_bias,
           gdn_norm, rel_bias, w_out_mix, ffn2_norm, ffn2_w_in, ffn2_w_out, final_norm):
    bsz, seq, _ = x.shape
    n_tok = bsz * seq
    gdn_rows = GDN_BLOCK * GDN_STEP_BLOCKS
    att_rows = ATT_TQ * ATT_STEP_TILES
    assert seq % ROW_TILE == 0 and seq % gdn_rows == 0 and seq % att_rows == 0 and n_tok % OUT_ROW_TILE == 0
    n_row_tiles = n_tok // ROW_TILE
    row = lambda i: (i, 0)
    params = functools.partial(pltpu.CompilerParams, vmem_limit_bytes=VMEM_LIMIT)

    gq = 3 * D_GROUP
    w_mix = w_in_mix.astype(BF16)
    w_main = jnp.concatenate([w_mix[:, 0:4 * D_GROUP], w_mix[:, 4 * D_GROUP + 2 * N_HEADS:]], axis=1)
    w_ab = jnp.pad(w_mix[:, 4 * D_GROUP:4 * D_GROUP + 2 * N_HEADS], ((0, 0), (0, LANES - 2 * N_HEADS)))

    cur = lambda i: (jnp.minimum(i, n_row_tiles - 1), 0)
    prev = lambda i: (jnp.maximum(i - 1, 0), 0)
    x1, gdn_qkv, gdn_z, ab, att_in = pl.pallas_call(
        functools.partial(_ffn1_proj_kernel, seq // ROW_TILE),
        grid=(n_row_tiles + 1,),
        in_specs=[pl.BlockSpec((ROW_TILE, D_MODEL), cur),
                  _resident((1, D_MODEL)), _resident((D_MODEL, 2 * D_FF)), _resident((D_FF, D_MODEL)),
                  _resident((1, D_MODEL)), _resident((D_MODEL, 7 * D_GROUP)), _resident((D_MODEL, LANES)),
                  _resident((CONV_WIDTH, gq))],
        out_specs=[pl.BlockSpec((ROW_TILE, D_MODEL), cur), pl.BlockSpec((ROW_TILE, gq), prev),
                   pl.BlockSpec((ROW_TILE, D_GROUP), cur), pl.BlockSpec((ROW_TILE, LANES), cur),
                   pl.BlockSpec((ROW_TILE, gq), cur)],
        out_shape=[jax.ShapeDtypeStruct((n_tok, D_MODEL), F32), jax.ShapeDtypeStruct((n_tok, gq), BF16),
                   jax.ShapeDtypeStruct((n_tok, D_GROUP), BF16), jax.ShapeDtypeStruct((n_tok, LANES), F32),
                   jax.ShapeDtypeStruct((n_tok, gq), BF16)],
        scratch_shapes=[pltpu.VMEM((ROW_TILE, D_FF), BF16), pltpu.VMEM((ROW_TILE + CONV_HALO, gq), F32)],
        compiler_params=params(dimension_semantics=("arbitrary",)),
        name="ffn1_proj",
    )(x.reshape(n_tok, D_MODEL), ffn1_norm.reshape(1, D_MODEL).astype(F32),
      ffn1_w_in.astype(BF16), ffn1_w_out.astype(BF16),
      mix_norm.reshape(1, D_MODEL).astype(F32), w_main, w_ab, conv_w.astype(F32))

    n = GDN_BLOCK
    head_of_lane = jnp.arange(D_GROUP) // HEAD_DIM
    lane = jnp.arange(LANES)
    tri = (jnp.arange(n)[:, None] >= jnp.arange(n)[None, :]).astype(BF16)
    e_a = (lane[:, None] == head_of_lane[None, :]).astype(BF16)
    e_b = (lane[:, None] == (head_of_lane + N_HEADS)[None, :]).astype(BF16)
    ones_bd = (lane[:, None] // HEAD_DIM == lane[None, :] // HEAD_DIM).astype(BF16)
    a_row = jnp.pad(jnp.exp(A_log.astype(F32)), (0, LANES - N_HEADS)).reshape(1, LANES)
    dt_row = jnp.pad(dt_bias.astype(F32), (0, LANES - N_HEADS)).reshape(1, LANES)
    gnorm_row = jnp.tile(gdn_norm.astype(F32), N_HEADS).reshape(1, D_GROUP)
    blk3 = lambda b, i: (b, i, 0)

    y_gdn = pl.pallas_call(
        _gdn_kernel,
        grid=(bsz, seq // gdn_rows),
        in_specs=[pl.BlockSpec((None, gdn_rows, gq), blk3), pl.BlockSpec((None, gdn_rows, D_GROUP), blk3),
                  pl.BlockSpec((None, gdn_rows, LANES), blk3),
                  _resident((1, LANES)), _resident((1, LANES)), _resident((1, D_GROUP)), _resident((n, n)),
                  _resident((LANES, D_GROUP)), _resident((LANES, D_GROUP)), _resident((LANES, LANES))],
        out_specs=pl.BlockSpec((None, gdn_rows, D_GROUP), blk3),
        out_shape=jax.ShapeDtypeStruct((bsz, seq, D_GROUP), BF16),
        scratch_shapes=[pltpu.VMEM((N_PAIRS, LANES, LANES), F32)],
        compiler_params=params(dimension_semantics=("arbitrary", "arbitrary")),
        name="gated_deltanet",
    )(gdn_qkv.reshape(bsz, seq, gq), gdn_z.reshape(bsz, seq, D_GROUP), ab.reshape(bsz, seq, LANES),
      a_row, dt_row, gnorm_row, tri, e_a, e_b, ones_bd)

    att3 = att_in.reshape(bsz, seq, gq)
    y_att = pl.pallas_call(
        _attn_kernel,
        grid=(bsz, seq // att_rows),
        in_specs=[pl.BlockSpec((None, att_rows, D_GROUP), lambda b, i: (b, i, 0)),
                  pl.BlockSpec((None, att_rows, D_GROUP), lambda b, i: (b, jnp.maximum(i - 1, 0), 1)),
                  pl.BlockSpec((None, att_rows, D_GROUP), lambda b, i: (b, i, 1)),
                  pl.BlockSpec((None, att_rows, D_GROUP), lambda b, i: (b, jnp.maximum(i - 1, 0), 2)),
                  pl.BlockSpec((None, att_rows, D_GROUP), lambda b, i: (b, i, 2)),
                  _resident((N_PAIRS, 2 * ATT_TQ, ATT_WIN))],
        out_specs=pl.BlockSpec((None, att_rows, D_GROUP), blk3),
        out_shape=jax.ShapeDtypeStruct((bsz, seq, D_GROUP), BF16),
        compiler_params=params(dimension_semantics=("arbitrary", "arbitrary")),
        name="chunk_attention",
    )(att3, att3, att3, att3, att3, _attention_bias(rel_bias))

    out = pl.pallas_call(
        _out_ffn2_kernel,
        grid=(n_tok // OUT_ROW_TILE,),
        in_specs=[pl.BlockSpec((OUT_ROW_TILE, D_MODEL), row), pl.BlockSpec((OUT_ROW_TILE, D_GROUP), row),
                  pl.BlockSpec((OUT_ROW_TILE, D_GROUP), row), _resident((2 * D_GROUP, D_MODEL)),
                  _resident((1, D_MODEL)), _resident((D_MODEL, 2 * D_FF)), _resident((D_FF, D_MODEL)),
                  _resident((1, D_MODEL))],
        out_specs=pl.BlockSpec((OUT_ROW_TILE, D_MODEL), row),
        out_shape=jax.ShapeDtypeStruct((n_tok, D_MODEL), F32),
        scratch_shapes=[pltpu.VMEM((OUT_ROW_TILE, D_FF), BF16)],
        compiler_params=params(dimension_semantics=("arbitrary",)),
        name="out_ffn2_norm",
    )(x1, y_gdn.reshape(n_tok, D_GROUP), y_att.reshape(n_tok, D_GROUP), w_out_mix.astype(BF16),
      ffn2_norm.reshape(1, D_MODEL).astype(F32), ffn2_w_in.astype(BF16), ffn2_w_out.astype(BF16),
      final_norm.reshape(1, D_MODEL).astype(F32))
    return out.reshape(bsz, seq, D_MODEL)


def kernel(x, ffn1_norm, ffn1_w_in, ffn1_w_out, mix_norm, w_in_mix, conv_w, A_log, dt_bias, gdn_norm,
           rel_bias, w_out_mix, ffn2_norm, ffn2_w_in, ffn2_w_out, final_norm):
    assert ffn1_norm.shape[0] == 1, "the final norm is fused into the single layer's last call"
    return _layer(x, ffn1_norm[0], ffn1_w_in[0], ffn1_w_out[0], mix_norm[0], w_in_mix[0], conv_w[0],
                  A_log[0], dt_bias[0], gdn_norm[0], rel_bias[0], w_out_mix[0], ffn2_norm[0],
                  ffn2_w_in[0], ffn2_w_out[0], final_norm)
```

```python
import functools

import jax
import jax.numpy as jnp
from jax import lax
from jax.experimental import pallas as pl
from jax.experimental.pallas import tpu as pltpu

F32 = jnp.float32
BF16 = jnp.bfloat16

D_MODEL = 1024
HEAD_DIM = 64
N_HEADS = 8
D_GROUP = N_HEADS * HEAD_DIM
D_FF = 2816
CONV_WIDTH = 4
ATT_CHUNK = 64
BAND_CHUNKS = 8
REL_CLIP = 128
RMS_EPS = 1e-6
L2_EPS = 1e-6
NEG_INF = -1e30

LANES = 128
SUBLANES = 8
N_PAIRS = D_GROUP // LANES
MXU_COLS = 256
FF_TILE = MXU_COLS
N_FF_TILES = D_FF // FF_TILE
ROW_TILE = 512
OUT_ROW_TILE = 1024
GDN_BLOCK = 128
GDN_STEP_BLOCKS = 8
GDN_GROUP_BLOCKS = 2
ATT_TQ = 128
ATT_BACK = BAND_CHUNKS * ATT_CHUNK
ATT_WIN = ATT_BACK + ATT_TQ
ATT_STEP_TILES = ATT_BACK // ATT_TQ
ATT_SCORES_AHEAD = 4
LOG2E = 1.4426950408889634


def _bias_free_tiles():
    free = []
    for w in range(ATT_WIN // LANES):
        k_lo, k_hi = w * LANES - ATT_BACK, w * LANES - ATT_BACK + LANES - 1
        clipped = 0 - k_hi >= REL_CLIP
        dchunks = [qc - kc for qc in range(ATT_TQ // ATT_CHUNK) for kc in (k_lo // ATT_CHUNK, k_hi // ATT_CHUNK)]
        if clipped and min(dchunks) >= 0 and max(dchunks) <= BAND_CHUNKS:
            free.append(w)
    return tuple(free)


ATT_BIAS_FREE_TILES = _bias_free_tiles()
CONV_HALO = SUBLANES
VMEM_LIMIT = 56 * 1024 * 1024


def _dot(a, b):
    return jnp.dot(a, b, preferred_element_type=F32)


def _dot_nt(a, b):
    return lax.dot_general(a, b, (((1,), (1,)), ((), ())), preferred_element_type=F32)


def _dot_tn(a, b):
    return lax.dot_general(a, b, (((0,), (0,)), ((), ())), preferred_element_type=F32)


def _bf16_pieces(x, n_pieces):
    pieces = []
    for _ in range(n_pieces - 1):
        hi = x.astype(BF16)
        pieces.append(hi)
        x = x - hi.astype(F32)
    pieces.append(x.astype(BF16))
    return pieces


def _select_cols(x, sel, n_pieces=3):
    return functools.reduce(lambda a, b: a + b, [_dot(p, sel) for p in _bf16_pieces(x, n_pieces)])


def _select_rows(sel, x, n_pieces=3):
    return functools.reduce(lambda a, b: a + b, [_dot(sel, p) for p in _bf16_pieces(x, n_pieces)])


def _rmsnorm(x, gain):
    ms = jnp.mean(x * x, axis=-1, keepdims=True)
    return x * lax.rsqrt(ms + RMS_EPS) * gain


def _silu(x):
    return x * (1.0 / (1.0 + jnp.exp(-x)))


def _softplus(x):
    return jnp.maximum(x, 0.0) + jnp.log(1.0 + jnp.exp(-jnp.abs(x)))


def _by_head(first_head, x):
    z = jnp.zeros_like(x)
    return jnp.where(first_head, x, z), jnp.where(first_head, z, x)


def _ffn_half_step(x, gain_ref, win_ref, wout_ref, act_ref):
    h = _rmsnorm(x, gain_ref[...]).astype(BF16)
    for j in range(N_FF_TILES):
        lo = j * FF_TILE
        gate = _dot(h, win_ref[:, lo:lo + FF_TILE])
        up = _dot(h, win_ref[:, D_FF + lo:D_FF + lo + FF_TILE])
        act_ref[:, lo:lo + FF_TILE] = (_silu(gate) * up).astype(BF16)
    return x + 0.5 * _dot(act_ref[...], wout_ref[...])


def _ffn1_proj_kernel(tiles_per_seq, x_ref, g1_ref, win_ref, wout_ref, gm_ref, wmix_ref, wab_ref, convw_ref,
                      x1_ref, qkv_ref, z_ref, ab_ref, att_ref, act_ref, pre_ref):
    i = pl.program_id(0)
    rows = x_ref.shape[0]
    d3 = 3 * D_GROUP

    @pl.when(i == 0)
    def _():
        pre_ref[...] = jnp.zeros_like(pre_ref)

    @pl.when((i + tiles_per_seq - 1) % tiles_per_seq == 0)
    def _():
        pre_ref[0:CONV_HALO, :] = jnp.zeros((CONV_HALO, d3), F32)

    for t in range(3):
        lo = t * D_GROUP
        xp = pre_ref[:, lo:lo + D_GROUP]
        acc = convw_ref[0:1, lo:lo + D_GROUP] * xp
        for j in range(1, CONV_WIDTH):
            acc = pltpu.roll(acc, 1, axis=0) + convw_ref[j:j + 1, lo:lo + D_GROUP] * xp
        qkv_ref[:, lo:lo + D_GROUP] = _silu(acc[CONV_HALO:CONV_HALO + rows]).astype(BF16)
    pre_ref[0:CONV_HALO, :] = pre_ref[rows:rows + CONV_HALO, :]

    x1 = _ffn_half_step(x_ref[...], g1_ref, win_ref, wout_ref, act_ref)
    x1_ref[...] = x1
    h = _rmsnorm(x1, gm_ref[...]).astype(BF16)
    for t in range(3):
        lo = t * D_GROUP
        pre_ref[CONV_HALO:CONV_HALO + rows, lo:lo + D_GROUP] = _dot(h, wmix_ref[:, lo:lo + D_GROUP])
    z_ref[...] = _dot(h, wmix_ref[:, d3:d3 + D_GROUP]).astype(BF16)
    ab_ref[...] = _dot(h, wab_ref[...])
    for t in range(3):
        lo = t * D_GROUP
        r = _dot(h, wmix_ref[:, 4 * D_GROUP + lo:4 * D_GROUP + lo + D_GROUP])
        if t == 0:
            r = r * (HEAD_DIM ** -0.5 * LOG2E)
        att_ref[:, lo:lo + D_GROUP] = r.astype(BF16)


def _out_ffn2_kernel(x1_ref, yg_ref, ya_ref, wo_ref, g2_ref, win_ref, wout_ref, gf_ref,
                     o_ref, act_ref):
    x2 = (x1_ref[...] + _dot(yg_ref[...], wo_ref[0:D_GROUP, :])
          + _dot(ya_ref[...], wo_ref[D_GROUP:2 * D_GROUP, :]))
    x3 = _ffn_half_step(x2, g2_ref, win_ref, wout_ref, act_ref)
    o_ref[...] = _rmsnorm(x3, gf_ref[...])


def _attention_tiles(q_ref, bias_ref, o_ref, windows):
    first_head = lax.broadcasted_iota(jnp.int32, (1, LANES), 1) < HEAD_DIM
    units = [(t, p) for t in range(len(windows)) for p in range(N_PAIRS)]

    def scores(t, p):
        load_k, _, n_keys = windows[t]
        lo = p * LANES
        q2 = q_ref[t * ATT_TQ:(t + 1) * ATT_TQ, lo:lo + LANES]
        qs = jnp.concatenate(_by_head(first_head, q2), axis=0)
        s = _dot_nt(qs, load_k(lo))
        cols = []
        for w in range((ATT_WIN - n_keys) // LANES, ATT_WIN // LANES):
            c0 = w * LANES - (ATT_WIN - n_keys)
            sw = s[:, c0:c0 + LANES]
            if w not in ATT_BIAS_FREE_TILES:
                sw = sw + bias_ref[p, :, w * LANES:(w + 1) * LANES]
            cols.append(sw)
        return jnp.concatenate(cols, axis=1)

    pending = [scores(*u) for u in units[:ATT_SCORES_AHEAD]]
    for idx, (t, p) in enumerate(units):
        s = pending.pop(0)
        if idx + ATT_SCORES_AHEAD < len(units):
            pending.append(scores(*units[idx + ATT_SCORES_AHEAD]))
        lo = p * LANES
        m = jnp.max(s, axis=-1, keepdims=True)
        e = jnp.exp2(s - m)
        denom = jnp.sum(e, axis=-1, keepdims=True)
        pv = _dot(e.astype(BF16), windows[t][1](lo)) * (1.0 / denom)
        o_ref[t * ATT_TQ:(t + 1) * ATT_TQ, lo:lo + LANES] = jnp.where(
            first_head, pv[0:ATT_TQ], pv[ATT_TQ:2 * ATT_TQ]).astype(BF16)


def _attn_kernel(q_ref, kp_ref, kc_ref, vp_ref, vc_ref, bias_ref, o_ref):
    step = pl.program_id(1)

    @pl.when(step == 0)
    def _():
        windows = [((lambda lo, n=(t + 1) * ATT_TQ: kc_ref[0:n, lo:lo + LANES]),
                    (lambda lo, n=(t + 1) * ATT_TQ: vc_ref[0:n, lo:lo + LANES]),
                    (t + 1) * ATT_TQ) for t in range(ATT_STEP_TILES)]
        _attention_tiles(q_ref, bias_ref, o_ref, windows)

    @pl.when(step > 0)
    def _():
        def window(prev_ref, cur_ref, t):
            return lambda lo: jnp.concatenate([prev_ref[t * ATT_TQ:ATT_BACK, lo:lo + LANES],
                                               cur_ref[0:(t + 1) * ATT_TQ, lo:lo + LANES]], axis=0)
        windows = [(window(kp_ref, kc_ref, t), window(vp_ref, vc_ref, t), ATT_WIN) for t in range(ATT_STEP_TILES)]
        _attention_tiles(q_ref, bias_ref, o_ref, windows)


def _block_diag2(a, b):
    z = jnp.zeros_like(a)
    return jnp.concatenate([jnp.concatenate([a, z], axis=1), jnp.concatenate([z, b], axis=1)], axis=0)


def _inverse_stages(chains):
    n = GDN_BLOCK
    row = lax.broadcasted_iota(jnp.int32, (n, 2 * n), 0)
    col = lax.broadcasted_iota(jnp.int32, (n, 2 * n), 1) % n

    def init():
        eye = (row == col).astype(F32)
        base = (row // 2 == col // 2)
        for c in chains:
            c["t2"] = (eye - jnp.where(base, c["l2"], 0.0)).astype(BF16)
            c["lb"] = c["l2"].astype(BF16)

    def level(b):
        halves = b % (2 * SUBLANES) == 0
        segs = [(s, s + b) for s in range(b, n, 2 * b)]

        def take(a):
            return a if not halves else jnp.concatenate([a[s:e] for s, e in segs], axis=0)

        def put(full, part):
            if not halves:
                return part
            out, at, r = [], 0, 0
            for s, e in segs:
                out += [full[at:s], part[r:r + e - s]]
                at, r = e, r + e - s
            return jnp.concatenate(out, axis=0)

        def cx():
            cmask = ((row // (2 * b) == col // (2 * b)) & (row % (2 * b) >= b) & (col % (2 * b) < b)).astype(BF16)
            for c in chains:
                x = c["t2"]
                y = _dot(take(c["lb"] * cmask), _block_diag2(x[:, 0:n], x[:, n:2 * n])).astype(BF16)
                c["y"] = put(jnp.zeros((n, 2 * n), BF16), y)

        def xcx():
            for c in chains:
                x, y = c["t2"], c.pop("y")
                z = _dot(take(x), _block_diag2(y[:, 0:n], y[:, n:2 * n]))
                c["t2"] = put(x, take(x) - z.astype(BF16))

        return [cx, xcx]

    stages, b = [init], 2
    while b < n:
        stages += level(b)
        b *= 2
    return stages


def _gdn_stages(block_ids, qkv_ref, z_ref, ab_ref, arow_ref, dtrow_ref, gnorm_ref, tri_ref, ea_ref, eb_ref,
                ones_ref, y_ref, state_ref):
    n = GDN_BLOCK
    row2 = lax.broadcasted_iota(jnp.int32, (n, 2 * n), 0)
    col2 = lax.broadcasted_iota(jnp.int32, (n, 2 * n), 1) % n
    first_head = lax.broadcasted_iota(jnp.int32, (1, LANES), 1) < HEAD_DIM
    chains = []

    def gating():
        for b in block_ids:
            r0 = b * n
            ab = ab_ref[r0:r0 + n, :]
            g_n = -arow_ref[...] * _softplus(ab + dtrow_ref[...])
            beta_n = 1.0 / (1.0 + jnp.exp(-ab))
            gc_n = _select_rows(tri_ref[...], g_n)
            gc_x = _select_cols(gc_n, ea_ref[...])
            g_last = gc_x[n - 1:n, :]
            blk = dict(r0=r0, gc_t=gc_n.T, gc_x=gc_x, beta_x=_select_cols(beta_n, eb_ref[...], n_pieces=2),
                       exp_gc=jnp.exp(gc_x), exp_rest=jnp.exp(g_last - gc_x), exp_last=jnp.exp(g_last))
            for p in range(N_PAIRS):
                chains.append(dict(blk=blk, p=p, lo=p * LANES))

    def squares():
        ones_bd = ones_ref[...]
        for c in chains:
            lo, r0 = c["lo"], c["blk"]["r0"]
            c["q"] = qkv_ref[r0:r0 + n, lo:lo + LANES].astype(F32)
            c["k"] = qkv_ref[r0:r0 + n, D_GROUP + lo:D_GROUP + lo + LANES].astype(F32)
            c["sq"] = _dot(jnp.concatenate([c["q"] * c["q"], c["k"] * c["k"]], axis=0).astype(BF16), ones_bd)

    def token_products():
        for c in chains:
            blk, lo, r0 = c["blk"], c["lo"], c["blk"]["r0"]
            sq = c.pop("sq")
            q = c.pop("q") * lax.rsqrt(sq[0:n] + L2_EPS) * (HEAD_DIM ** -0.5)
            k = c.pop("k") * lax.rsqrt(sq[n:2 * n] + L2_EPS)
            v = qkv_ref[r0:r0 + n, 2 * D_GROUP + lo:2 * D_GROUP + lo + LANES].astype(F32)
            beta = blk["beta_x"][:, lo:lo + LANES]
            kb = k * beta
            c["vb"] = (v * beta).astype(BF16)
            c["kw"] = (kb * blk["exp_gc"][:, lo:lo + LANES]).astype(BF16)
            c["q_dec"] = (q * blk["exp_gc"][:, lo:lo + LANES]).astype(BF16)
            c["k_rest"] = (k * blk["exp_rest"][:, lo:lo + LANES]).astype(BF16)
            kf = k.astype(BF16)
            c["tt"] = _dot_nt(jnp.concatenate([kb.astype(BF16), q.astype(BF16)], axis=0),
                              jnp.concatenate(_by_head(first_head, kf), axis=0))

    def decays():
        lower2 = row2 >= col2
        strict2 = row2 > col2
        for c in chains:
            blk, p, lo = c["blk"], c["p"], c["lo"]
            g2 = blk["gc_x"][:, lo:lo + LANES]
            g2_swapped = pltpu.roll(g2, HEAD_DIM, axis=1)
            gcol = jnp.concatenate([jnp.where(first_head, g2, g2_swapped),
                                    jnp.where(first_head, g2_swapped, g2)], axis=1)
            grow = jnp.concatenate([jnp.broadcast_to(blk["gc_t"][2 * p:2 * p + 1, :], (n, n)),
                                    jnp.broadcast_to(blk["gc_t"][2 * p + 1:2 * p + 2, :], (n, n))], axis=1)
            dec = jnp.exp(jnp.minimum(gcol - grow, 0.0))
            tt = c.pop("tt")
            c["l2"] = jnp.where(strict2, tt[0:n] * dec, 0.0)
            c["aqk"] = jnp.where(lower2, tt[n:2 * n] * dec, 0.0).astype(BF16)

    def solve():
        for c in chains:
            vb0, vb1 = _by_head(first_head, c.pop("vb"))
            kw0, kw1 = _by_head(first_head, c.pop("kw"))
            rhs = jnp.concatenate([jnp.concatenate([vb0, kw0], axis=1),
                                   jnp.concatenate([vb1, kw1], axis=1)], axis=0)
            c["uw"] = _dot(c.pop("t2"), rhs)
            c.pop("l2"), c.pop("lb")

    def state_in(i):
        def run():
            for c in chains[i * N_PAIRS:(i + 1) * N_PAIRS]:
                c["state"] = state_ref[c["p"]]
                w = c["uw"][:, LANES:2 * LANES].astype(BF16)
                c["ws"] = _dot(jnp.concatenate([w, c.pop("q_dec")], axis=0), c["state"].astype(BF16))
        return run

    def state_out(i):
        def run():
            head_block = (lax.broadcasted_iota(jnp.int32, (LANES, LANES), 0) // HEAD_DIM
                          == lax.broadcasted_iota(jnp.int32, (LANES, LANES), 1) // HEAD_DIM)
            for c in chains[i * N_PAIRS:(i + 1) * N_PAIRS]:
                lo = c["lo"]
                ws = c.pop("ws")
                v_new = (c.pop("uw")[:, 0:LANES] - ws[0:n]).astype(BF16)
                upd = _dot_tn(c.pop("k_rest"), v_new)
                intra = _dot(c.pop("aqk"), jnp.concatenate(_by_head(first_head, v_new), axis=0))
                state_ref[c["p"]] = (c.pop("state") * c["blk"]["exp_last"][:, lo:lo + LANES]
                                     + jnp.where(head_block, upd, 0.0))
                c["o"] = ws[n:2 * n] + intra
        return run

    def mean_squares():
        ones_bd = ones_ref[...]
        for c in chains:
            c["ms"] = _dot((c["o"] * c["o"]).astype(BF16), ones_bd) * (1.0 / HEAD_DIM)

    def outputs():
        for c in chains:
            lo, r0 = c["lo"], c["blk"]["r0"]
            z = z_ref[r0:r0 + n, lo:lo + LANES].astype(F32)
            y = c.pop("o") * lax.rsqrt(c.pop("ms") + RMS_EPS) * gnorm_ref[:, lo:lo + LANES] * _silu(z)
            y_ref[r0:r0 + n, lo:lo + LANES] = y.astype(BF16)

    prep = [gating, squares, token_products, decays]
    tail = [solve]
    for i in range(len(block_ids)):
        tail += [state_in(i), state_out(i)]
    return prep, _inverse_stages(chains), tail + [mean_squares, outputs]


def _interleave(a, b):
    out, ia, ib = [], 0, 0
    while ia < len(a) or ib < len(b):
        if ib >= len(b) or (ia < len(a) and ia * len(b) <= ib * len(a)):
            out.append(a[ia])
            ia += 1
        else:
            out.append(b[ib])
            ib += 1
    return out


def _gdn_kernel(qkv_ref, z_ref, ab_ref, arow_ref, dtrow_ref, gnorm_ref, tri_ref, ea_ref, eb_ref, ones_ref,
                y_ref, state_ref):
    @pl.when(pl.program_id(1) == 0)
    def _():
        state_ref[...] = jnp.zeros_like(state_ref)

    refs = (qkv_ref, z_ref, ab_ref, arow_ref, dtrow_ref, gnorm_ref, tri_ref, ea_ref, eb_ref, ones_ref,
            y_ref, state_ref)
    n_groups = GDN_STEP_BLOCKS // GDN_GROUP_BLOCKS
    groups = [_gdn_stages(tuple(range(g * GDN_GROUP_BLOCKS, (g + 1) * GDN_GROUP_BLOCKS)), *refs)
              for g in range(n_groups)]
    order = list(groups[0][0])
    for g in range(n_groups):
        beside = list(groups[g - 1][2]) if g > 0 else []
        if g + 1 < n_groups:
            beside = _interleave(beside, groups[g + 1][0]) if beside else list(groups[g + 1][0])
        order += _interleave(groups[g][1], beside)
    order += groups[-1][2]
    for stage in order:
        stage()


def _resident(shape):
    zeros = (0,) * len(shape)
    return pl.BlockSpec(shape, lambda *_: zeros, pipeline_mode=pl.Buffered(1))


def _attention_bias(rel_bias):
    span = ATT_WIN + ATT_TQ - 1
    rel = (ATT_WIN - 1) - jnp.arange(span)
    u = rel_bias.astype(F32)[:, jnp.clip(rel, -REL_CLIP, REL_CLIP) + REL_CLIP]
    u = jnp.pad(u, ((0, 0), (0, 1)))
    skew = jnp.tile(u, (1, ATT_TQ))[:, :ATT_TQ * span].reshape(N_HEADS, ATT_TQ, span)
    bias = skew[:, :, ATT_TQ - 1:ATT_TQ - 1 + ATT_WIN]
    bias = (bias - rel_bias.astype(F32)[:, 2 * REL_CLIP][:, None, None]) * LOG2E
    qpos = jnp.arange(ATT_TQ)[:, None]
    kpos = jnp.arange(ATT_WIN)[None, :] - ATT_BACK
    dchunk = qpos // ATT_CHUNK - kpos // ATT_CHUNK
    valid = (dchunk >= 0) & (dchunk <= BAND_CHUNKS)
    return jnp.where(valid[None], bias, NEG_INF).reshape(N_PAIRS, 2 * ATT_TQ, ATT_WIN)


def _layer(x, ffn1_norm, ffn1_w_in, ffn1_w_out, mix_norm, w_in_mix, conv_w, A_log, dt_bias,
           gdn_norm, rel_bias, w_out_mix, ffn2_norm, ffn2_w_in, ffn2_w_out, final_norm):
    bsz, seq, _ = x.shape
    n_tok = bsz * seq
    gdn_rows = GDN_BLOCK * GDN_STEP_BLOCKS
    att_rows = ATT_TQ * ATT_STEP_TILES
    assert seq % ROW_TILE == 0 and seq % gdn_rows == 0 and seq % att_rows == 0 and n_tok % OUT_ROW_TILE == 0
    n_row_tiles = n_tok // ROW_TILE
    row = lambda i: (i, 0)
    params = functools.partial(pltpu.CompilerParams, vmem_limit_bytes=VMEM_LIMIT)

    gq = 3 * D_GROUP
    w_mix = w_in_mix.astype(BF16)
    w_main = jnp.concatenate([w_mix[:, 0:4 * D_GROUP], w_mix[:, 4 * D_GROUP + 2 * N_HEADS:]], axis=1)
    w_ab = jnp.pad(w_mix[:, 4 * D_GROUP:4 * D_GROUP + 2 * N_HEADS], ((0, 0), (0, LANES - 2 * N_HEADS)))

    cur = lambda i: (jnp.minimum(i, n_row_tiles - 1), 0)
    prev = lambda i: (jnp.maximum(i - 1, 0), 0)
    x1, gdn_qkv, gdn_z, ab, att_in = pl.pallas_call(
        functools.partial(_ffn1_proj_kernel, seq // ROW_TILE),
        grid=(n_row_tiles + 1,),
        in_specs=[pl.BlockSpec((ROW_TILE, D_MODEL), cur),
                  _resident((1, D_MODEL)), _resident((D_MODEL, 2 * D_FF)), _resident((D_FF, D_MODEL)),
                  _resident((1, D_MODEL)), _resident((D_MODEL, 7 * D_GROUP)), _resident((D_MODEL, LANES)),
                  _resident((CONV_WIDTH, gq))],
        out_specs=[pl.BlockSpec((ROW_TILE, D_MODEL), cur), pl.BlockSpec((ROW_TILE, gq), prev),
                   pl.BlockSpec((ROW_TILE, D_GROUP), cur), pl.BlockSpec((ROW_TILE, LANES), cur),
                   pl.BlockSpec((ROW_TILE, gq), cur)],
        out_shape=[jax.ShapeDtypeStruct((n_tok, D_MODEL), F32), jax.ShapeDtypeStruct((n_tok, gq), BF16),
                   jax.ShapeDtypeStruct((n_tok, D_GROUP), BF16), jax.ShapeDtypeStruct((n_tok, LANES), F32),
                   jax.ShapeDtypeStruct((n_tok, gq), BF16)],
        scratch_shapes=[pltpu.VMEM((ROW_TILE, D_FF), BF16), pltpu.VMEM((ROW_TILE + CONV_HALO, gq), F32)],
        compiler_params=params(dimension_semantics=("arbitrary",)),
        name="ffn1_proj",
    )(x.reshape(n_tok, D_MODEL), ffn1_norm.reshape(1, D_MODEL).astype(F32),
      ffn1_w_in.astype(BF16), ffn1_w_out.astype(BF16),
      mix_norm.reshape(1, D_MODEL).astype(F32), w_main, w_ab, conv_w.astype(F32))

    n = GDN_BLOCK
    head_of_lane = jnp.arange(D_GROUP) // HEAD_DIM
    lane = jnp.arange(LANES)
    tri = (jnp.arange(n)[:, None] >= jnp.arange(n)[None, :]).astype(BF16)
    e_a = (lane[:, None] == head_of_lane[None, :]).astype(BF16)
    e_b = (lane[:, None] == (head_of_lane + N_HEADS)[None, :]).astype(BF16)
    ones_bd = (lane[:, None] // HEAD_DIM == lane[None, :] // HEAD_DIM).astype(BF16)
    a_row = jnp.pad(jnp.exp(A_log.astype(F32)), (0, LANES - N_HEADS)).reshape(1, LANES)
    dt_row = jnp.pad(dt_bias.astype(F32), (0, LANES - N_HEADS)).reshape(1, LANES)
    gnorm_row = jnp.tile(gdn_norm.astype(F32), N_HEADS).reshape(1, D_GROUP)
    blk3 = lambda b, i: (b, i, 0)

    y_gdn = pl.pallas_call(
        _gdn_kernel,
        grid=(bsz, seq // gdn_rows),
        in_specs=[pl.BlockSpec((None, gdn_rows, gq), blk3), pl.BlockSpec((None, gdn_rows, D_GROUP), blk3),
                  pl.BlockSpec((None, gdn_rows, LANES), blk3),
                  _resident((1, LANES)), _resident((1, LANES)), _resident((1, D_GROUP)), _resident((n, n)),
                  _resident((LANES, D_GROUP)), _resident((LANES, D_GROUP)), _resident((LANES, LANES))],
        out_specs=pl.BlockSpec((None, gdn_rows, D_GROUP), blk3),
        out_shape=jax.ShapeDtypeStruct((bsz, seq, D_GROUP), BF16),
        scratch_shapes=[pltpu.VMEM((N_PAIRS, LANES, LANES), F32)],
        compiler_params=params(dimension_semantics=("arbitrary", "arbitrary")),
        name="gated_deltanet",
    )(gdn_qkv.reshape(bsz, seq, gq), gdn_z.reshape(bsz, seq, D_GROUP), ab.reshape(bsz, seq, LANES),
      a_row, dt_row, gnorm_row, tri, e_a, e_b, ones_bd)

    att3 = att_in.reshape(bsz, seq, gq)
    y_att = pl.pallas_call(
        _attn_kernel,
        grid=(bsz, seq // att_rows),
        in_specs=[pl.BlockSpec((None, att_rows, D_GROUP), lambda b, i: (b, i, 0)),
                  pl.BlockSpec((None, att_rows, D_GROUP), lambda b, i: (b, jnp.maximum(i - 1, 0), 1)),
                  pl.BlockSpec((None, att_rows, D_GROUP), lambda b, i: (b, i, 1)),
                  pl.BlockSpec((None, att_rows, D_GROUP), lambda b, i: (b, jnp.maximum(i - 1, 0), 2)),
                  pl.BlockSpec((None, att_rows, D_GROUP), lambda b, i: (b, i, 2)),
                  _resident((N_PAIRS, 2 * ATT_TQ, ATT_WIN))],
        out_specs=pl.BlockSpec((None, att_rows, D_GROUP), blk3),
        out_shape=jax.ShapeDtypeStruct((bsz, seq, D_GROUP), BF16),
        compiler_params=params(dimension_semantics=("arbitrary", "arbitrary")),
        name="chunk_attention",
    )(att3, att3, att3, att3, att3, _attention_bias(rel_bias))

    out = pl.pallas_call(
        _out_ffn2_kernel,
        grid=(n_tok // OUT_ROW_TILE,),
        in_specs=[pl.BlockSpec((OUT_ROW_TILE, D_MODEL), row), pl.BlockSpec((OUT_ROW_TILE, D_GROUP), row),
                  pl.BlockSpec((OUT_ROW_TILE, D_GROUP), row), _resident((2 * D_GROUP, D_MODEL)),
                  _resident((1, D_MODEL)), _resident((D_MODEL, 2 * D_FF)), _resident((D_FF, D_MODEL)),
                  _resident((1, D_MODEL))],
        out_specs=pl.BlockSpec((OUT_ROW_TILE, D_MODEL), row),
        out_shape=jax.ShapeDtypeStruct((n_tok, D_MODEL), F32),
        scratch_shapes=[pltpu.VMEM((OUT_ROW_TILE, D_FF), BF16)],
        compiler_params=params(dimension_semantics=("arbitrary",)),
        name="out_ffn2_norm",
    )(x1, y_gdn.reshape(n_tok, D_GROUP), y_att.reshape(n_tok, D_GROUP), w_out_mix.astype(BF16),
      ffn2_norm.reshape(1, D_MODEL).astype(F32), ffn2_w_in.astype(BF16), ffn2_w_out.astype(BF16),
      final_norm.reshape(1, D_MODEL).astype(F32))
    return out.reshape(bsz, seq, D_MODEL)


def kernel(x, ffn1_norm, ffn1_w_in, ffn1_w_out, mix_norm, w_in_mix, conv_w, A_log, dt_bias, gdn_norm,
           rel_bias, w_out_mix, ffn2_norm, ffn2_w_in, ffn2_w_out, final_norm):
    assert ffn1_norm.shape[0] == 1, "the final norm is fused into the single layer's last call"
    return _layer(x, ffn1_norm[0], ffn1_w_in[0], ffn1_w_out[0], mix_norm[0], w_in_mix[0], conv_w[0],
                  A_log[0], dt_bias[0], gdn_norm[0], rel_bias[0], w_out_mix[0], ffn2_norm[0],
                  ffn2_w_in[0], ffn2_w_out[0], final_norm)
```
